```python
import jax, jax.numpy as jnp
from jax import lax
import numpy as np

D_MODEL = 1024
BATCH = 16
SEQ = 2048
DEPTH = 1
DEC_BATCH = 128
DEC_SEQ = 4
PAST_LEN = 16384
PAGE_SIZE = 128

N_MEM = 256
EPS = 1e-6
ROPE_THETA = 500000.0
Q_BLK = 128
NSA_HEADS = 8
NSA_KV = 2
NSA_HD = 64
NSA_ROT = NSA_HD // 4
NSA_BLK = 64
NSA_TOPK = 16
NSA_WINDOW = 512
NSA_SEL_QBLK = 32
NSA_SCALE = NSA_HD ** -0.5
MLA_HEADS = 8
MLA_QLORA = 384
MLA_KVLORA = 256
MLA_NOPE = 64
MLA_ROPE = 32
MLA_V = 64
MLA_SCALE = (MLA_NOPE + MLA_ROPE) ** -0.5
LATENT = MLA_KVLORA + MLA_ROPE
MEM_HEADS = 4
MEM_HD = 128
MEM_SCALE = MEM_HD ** -0.5

NSA_W = NSA_HEADS * NSA_HD
MLA_W = MLA_HEADS * MLA_V
MEM_W = MEM_HEADS * MEM_HD
N_BRANCH = 3
IN_SPLITS = (NSA_W, 3 * 2 * NSA_KV * NSA_HD, NSA_HEADS * 3, NSA_W,
             MLA_QLORA, MLA_KVLORA, MLA_ROPE, MLA_W,
             MEM_W, MEM_W, N_BRANCH * D_MODEL)
D_IN = sum(IN_SPLITS)
NEG = -1e30
TINY = 1e-30
SEL_BONUS = 1e3
F32 = jnp.float32

kernel_name = "nsa_mla_memory_gated_hybrid_step"


def rmsnorm(x, g):
    xf = x.astype(F32)
    y = xf * lax.rsqrt(jnp.mean(xf * xf, axis=-1, keepdims=True) + EPS)
    return (y * g.astype(F32)).astype(x.dtype)


def rope(x, pos, n_rot):
    half = n_rot // 2
    inv = ROPE_THETA ** (-jnp.arange(half, dtype=F32) / half)
    ang = pos.astype(F32)[:, None] * inv[None, :]
    cos, sin = jnp.cos(ang)[:, None, :], jnp.sin(ang)[:, None, :]
    xr = x[..., :n_rot].astype(F32)
    x1, x2 = xr[..., :half], xr[..., half:]
    rot = jnp.concatenate([x1 * cos - x2 * sin, x2 * cos + x1 * sin], axis=-1).astype(x.dtype)
    return jnp.concatenate([rot, x[..., n_rot:]], axis=-1)


def masked_softmax(s, mask):
    s = jnp.where(mask, s.astype(F32), NEG)
    p = jnp.where(mask, jnp.exp(s - jnp.max(s, axis=-1, keepdims=True)), 0.0)
    return p / jnp.maximum(jnp.sum(p, axis=-1, keepdims=True), TINY)


def front(x, pos, g_pre, w_in, mla_g_q, mla_w_uq, mla_g_kv, mla_w_uk):
    b, t, _ = x.shape
    h = rmsnorm(x, g_pre)
    offsets = np.cumsum(IN_SPLITS)[:-1].tolist()
    q, kv, bg, z_nsa, cq, ckv, kr, z_mla, q_mem, z_mem, gm = jnp.split(h @ w_in, offsets, axis=-1)
    q = rope(q.reshape(b, t, NSA_HEADS, NSA_HD), pos, NSA_ROT)
    kv = kv.reshape(b, t, 3, 2, NSA_KV, NSA_HD)
    k = rope(kv[:, :, :, 0].reshape(b, t, 3 * NSA_KV, NSA_HD), pos, NSA_ROT).reshape(b, t, 3, NSA_KV, NSA_HD)
    v = kv[:, :, :, 1]
    nsa_rows = jnp.stack([k[:, :, 0], v[:, :, 0], k[:, :, 1], v[:, :, 1]], axis=2)
    qm = (rmsnorm(cq, mla_g_q) @ mla_w_uq).reshape(b, t, MLA_HEADS, MLA_NOPE + MLA_ROPE)
    q_lat = jnp.einsum('bthn,hnc->bthc', qm[..., :MLA_NOPE], mla_w_uk)
    q_cat = jnp.concatenate([q_lat, rope(qm[..., MLA_NOPE:], pos, MLA_ROPE)], axis=-1)
    latent = jnp.concatenate([rmsnorm(ckv, mla_g_kv),
                              rope(kr[:, :, None, :], pos, MLA_ROPE)[:, :, 0]], axis=-1)
    return dict(q=q, nsa_rows=nsa_rows, kw=k[:, :, 2], vw=v[:, :, 2], bg=bg, z_nsa=z_nsa,
                q_cat=q_cat, latent=latent, z_mla=z_mla,
                q_mem=q_mem.reshape(b, t, MEM_HEADS, MEM_HD), z_mem=z_mem, gm=gm)


def nsa_compress(rows, w):
    b, n, g, d = rows.shape
    return jnp.einsum('bnlgd,lde->bnge', rows.reshape(b, n // NSA_BLK, NSA_BLK, g, d), w)


def nsa_cmp_select(q, kcb, vcb, qpos):
    b, t, h, d = q.shape
    nb = kcb.shape[1]
    qg = q.reshape(b, t, NSA_KV, h // NSA_KV, d)
    blk = jnp.arange(nb)
    vis = (blk[None, :] + 1) * NSA_BLK <= qpos[:, None] + 1
    s = jnp.einsum('btgrd,bngd->btgrn', qg, kcb) * NSA_SCALE
    p = masked_softmax(s, vis[None, :, None, None, :])
    o_cmp = jnp.einsum('btgrn,bngd->btgrd', p.astype(vcb.dtype), vcb).reshape(b, t, h, d)
    cur = (qpos // NSA_BLK)[:, None]
    valid = blk[None, :] <= cur
    forced = (blk[None, :] == 0) | (blk[None, :] == cur) | (blk[None, :] == cur - 1)
    score = jnp.sum(p, axis=3) + jnp.where(forced, SEL_BONUS, 0.0)[None, :, None, :]
    score = jnp.where(valid[None, :, None, :], score, -jnp.inf)
    _, sel = lax.top_k(score, min(NSA_TOPK, nb))
    return o_cmp, sel


def nsa_select_prompt(q, ks, vs, sel, qpos):
    b, t, h, d = q.shape
    r = h // NSA_KV
    n = sel.shape[-1]
    nb = t // NSA_BLK
    ksb = ks.reshape(b, nb, NSA_BLK, NSA_KV, d).transpose(0, 3, 1, 2, 4)
    vsb = vs.reshape(b, nb, NSA_BLK, NSA_KV, d).transpose(0, 3, 1, 2, 4)
    qc_len = min(NSA_SEL_QBLK, t)
    nc = t // qc_len
    bi = jnp.arange(b)[:, None, None, None]
    gi = jnp.arange(NSA_KV)[None, None, :, None]

    def chunk(args):
        qc, sc, pc = args
        kg = ksb[bi, gi, sc].reshape(b, qc_len, NSA_KV, n * NSA_BLK, d)
        vg = vsb[bi, gi, sc].reshape(b, qc_len, NSA_KV, n * NSA_BLK, d)
        kpos = (sc[..., None] * NSA_BLK + jnp.arange(NSA_BLK)).reshape(b, qc_len, NSA_KV, 1, n * NSA_BLK)
        s = jnp.einsum('bqgrd,bqgkd->bqgrk', qc, kg) * NSA_SCALE
        p = masked_softmax(s, kpos <= pc[None, :, None, None, None])
        return jnp.einsum('bqgrk,bqgkd->bqgrd', p.astype(vg.dtype), vg)

    qs = q.reshape(b, nc, qc_len, NSA_KV, r, d).swapaxes(0, 1)
    ss = sel.reshape(b, nc, qc_len, NSA_KV, n).swapaxes(0, 1)
    out = lax.map(chunk, (qs, ss, qpos.reshape(nc, qc_len)))
    return out.swapaxes(0, 1).reshape(b, t, h, d)


def nsa_window_prompt(q, kw, vw):
    b, t, h, d = q.shape
    r = h // NSA_KV
    qb_len = min(Q_BLK, t)
    nq = t // qb_len
    w = NSA_WINDOW
    pad = ((0, 0), (w, 0), (0, 0), (0, 0))
    kp, vp = jnp.pad(kw, pad), jnp.pad(vw, pad)
    a = jnp.arange(w + qb_len)
    dist = jnp.arange(qb_len)[:, None] + w - a[None, :]

    def block(args):
        i, qb = args
        start = i * qb_len
        kb = lax.dynamic_slice_in_dim(kp, start, w + qb_len, axis=1)
        vb = lax.dynamic_slice_in_dim(vp, start, w + qb_len, axis=1)
        mask = (dist >= 0) & (dist < w) & (start - w + a >= 0)[None, :]
        s = jnp.einsum('bqgrd,bkgd->bqgrk', qb, kb) * NSA_SCALE
        p = masked_softmax(s, mask[None, :, None, None, :])
        return jnp.einsum('bqgrk,bkgd->bqgrd', p.astype(vb.dtype), vb)

    qs = q.reshape(b, nq, qb_len, NSA_KV, r, d).swapaxes(0, 1)
    out = lax.map(block, (jnp.arange(nq), qs))
    return out.swapaxes(0, 1).reshape(b, t, h, d)


def nsa_cmp_past(pool, layer, page_table, w_ck, w_cv):
    b = page_table.shape[0]

    def page(ids):
        return (nsa_compress(pool[layer, ids, :, 0], w_ck), nsa_compress(pool[layer, ids, :, 1], w_cv))

    kc, vc = lax.map(page, page_table.T)
    return (kc.swapaxes(0, 1).reshape(b, -1, NSA_KV, NSA_HD),
            vc.swapaxes(0, 1).reshape(b, -1, NSA_KV, NSA_HD))


def nsa_select_sample(q, ks, vs, sel, pool, layer, page_table):
    b, t, h, d = q.shape
    r = h // NSA_KV
    n = sel.shape[-1]
    bpp = PAGE_SIZE // NSA_BLK
    nbp = page_table.shape[1] * bpp
    past_len = page_table.shape[1] * PAGE_SIZE
    qg = q.reshape(b, t, NSA_KV, r, d)
    sc = jnp.minimum(sel, nbp - 1)
    page = page_table[jnp.arange(b)[:, None, None, None], sc // bpp][..., None]
    tok = (sc % bpp)[..., None] * NSA_BLK + jnp.arange(NSA_BLK)
    gi = jnp.arange(NSA_KV)[None, None, :, None, None]
    kg = pool[layer, page, tok, 2, gi].reshape(b, t, NSA_KV, n * NSA_BLK, d)
    vg = pool[layer, page, tok, 3, gi].reshape(b, t, NSA_KV, n * NSA_BLK, d)
    m_past = jnp.repeat(sel < nbp, NSA_BLK, axis=-1)[:, :, :, None, :]
    kblk = (past_len + jnp.arange(t)) // NSA_BLK
    chosen = jnp.any(sel[..., None] == kblk, axis=-2)
    causal = jnp.arange(t)[None, :] <= jnp.arange(t)[:, None]
    m_new = (chosen & causal[None, :, None, :])[:, :, :, None, :]
    s = jnp.concatenate([jnp.einsum('btgrd,btgkd->btgrk', qg, kg),
                         jnp.einsum('btgrd,bagd->btgra', qg, ks)], axis=-1) * NSA_SCALE
    p = masked_softmax(s, jnp.concatenate([m_past, m_new], axis=-1)).astype(vg.dtype)
    o = (jnp.einsum('btgrk,btgkd->btgrd', p[..., :n * NSA_BLK], vg)
         + jnp.einsum('btgra,bagd->btgrd', p[..., n * NSA_BLK:], vs))
    return o.reshape(b, t, h, d)


def nsa_window_sample(q, kw, vw, win_kv, past_len):
    b, t, h, d = q.shape
    wb = win_kv.shape[1]
    qg = q.reshape(b, t, NSA_KV, h // NSA_KV, d)
    keys = jnp.concatenate([win_kv[:, :, 0], kw], axis=1)
    vals = jnp.concatenate([win_kv[:, :, 1], vw], axis=1)
    dist = (past_len + jnp.arange(t))[:, None] - (past_len - wb + jnp.arange(wb + t))[None, :]
    mask = (dist >= 0) & (dist < NSA_WINDOW)
    s = jnp.einsum('btgrd,bsgd->btgrs', qg, keys) * NSA_SCALE
    p = masked_softmax(s, mask[None, :, None, None, :]).astype(vals.dtype)
    o = jnp.einsum('btgrs,bsgd->btgrd', p, vals).reshape(b, t, h, d)
    return o, jnp.stack([keys[:, t:], vals[:, t:]], axis=2)


def nsa_merge(bg, o_c, o_s, o_w):
    b, t, h, d = o_c.shape
    g = jax.nn.sigmoid(bg.reshape(b, t, h, 3))
    return (g[..., 0:1] * o_c + g[..., 1:2] * o_s + g[..., 2:3] * o_w).reshape(b, t, h * d)


def mla_prompt(q_cat, latent, w_uv):
    b, t, h, c = q_cat.shape
    qb_len = min(Q_BLK, t)
    nq = t // qb_len
    kpos = jnp.arange(t)

    def block(args):
        i, qb = args
        qpos = i * qb_len + jnp.arange(qb_len)
        s = jnp.einsum('bqhc,bkc->bqhk', qb, latent) * MLA_SCALE
        p = masked_softmax(s, (kpos[None, :] <= qpos[:, None])[None, :, None, :])
        return jnp.einsum('bqhk,bkc->bqhc', p.astype(latent.dtype), latent[..., :MLA_KVLORA])

    qs = q_cat.reshape(b, nq, qb_len, h, c).swapaxes(0, 1)
    o_lat = lax.map(block, (jnp.arange(nq), qs)).swapaxes(0, 1).reshape(b, t, h, MLA_KVLORA)
    return jnp.einsum('bthc,hcv->bthv', o_lat, w_uv).reshape(b, t, h * MLA_V)


def mla_sample(q_cat, lat_new, pool, layer, page_table, w_uv):
    b, t, h, _ = q_cat.shape
    qf = q_cat.astype(F32)

    def update(carry, s, vals):
        m, l, acc = carry
        m_new = jnp.maximum(m, jnp.max(s, axis=-1))
        alpha = jnp.exp(m - m_new)
        p = jnp.exp(s - m_new[..., None])
        return (m_new, l * alpha + jnp.sum(p, axis=-1),
                acc * alpha[..., None] + jnp.einsum('bthk,bkc->bthc', p, vals))

    def page_step(carry, ids):
        rows = pool[layer, ids].astype(F32)
        s = jnp.einsum('bthc,bkc->bthk', qf, rows) * MLA_SCALE
        return update(carry, s, rows[..., :MLA_KVLORA]), None

    init = (jnp.full((b, t, h), NEG, F32), jnp.zeros((b, t, h), F32), jnp.zeros((b, t, h, MLA_KVLORA), F32))
    carry, _ = lax.scan(page_step, init, page_table.T)
    new = lat_new.astype(F32)
    causal = jnp.arange(t)[None, :] <= jnp.arange(t)[:, None]
    s = jnp.where(causal[None, :, None, :], jnp.einsum('bthc,bkc->bthk', qf, new) * MLA_SCALE, NEG)
    _, l, acc = update(carry, s, new[..., :MLA_KVLORA])
    o_lat = (acc / l[..., None]).astype(q_cat.dtype)
    return jnp.einsum('bthc,hcv->bthv', o_lat, w_uv).reshape(b, t, h * MLA_V)


def mem_attend(q, kv):
    b, t, h, d = q.shape
    s = jnp.einsum('bthd,bshd->bths', q, kv[:, :, 0]) * MEM_SCALE
    p = jax.nn.softmax(s.astype(F32), axis=-1).astype(kv.dtype)
    return jnp.einsum('bths,bshd->bthd', p, kv[:, :, 1]).reshape(b, t, h * d)


def back(x, f, o_nsa, o_mla, o_mem, w_br_nsa, w_br_mla, w_br_mem, w_out, g_post):
    b, t, _ = x.shape
    gate = jax.nn.sigmoid(f['gm'].reshape(b, t, N_BRANCH, D_MODEL))
    y = (gate[:, :, 0] * ((o_nsa * jax.nn.silu(f['z_nsa'])) @ w_br_nsa)
         + gate[:, :, 1] * ((o_mla * jax.nn.silu(f['z_mla'])) @ w_br_mla)
         + gate[:, :, 2] * ((o_mem * jax.nn.silu(f['z_mem'])) @ w_br_mem))
    return x + rmsnorm(y @ w_out, g_post)


def setup_inputs(seed: int = 0) -> dict:
    key = jax.random.key(seed)
    ks = jax.random.split(key, 24)
    n_pages = PAST_LEN // PAGE_SIZE
    n_used = DEC_BATCH * n_pages
    n_pool = n_used + n_used // 4
    wb = min(NSA_WINDOW, PAST_LEN)

    def nrm(k, shape, scale=1.0):
        return jax.random.normal(k, shape, F32) * scale

    def gain(k, shape):
        return 1.0 + 0.01 * jax.random.normal(k, shape, F32)

    page_table = jax.random.permutation(ks[6], n_pool)[:n_used].reshape(DEC_BATCH, n_pages).astype(jnp.int32)
    return {
        'x_prompt': nrm(ks[0], (BATCH, SEQ, D_MODEL)),
        'x_sample': nrm(ks[1], (DEC_BATCH, DEC_SEQ, D_MODEL)),
        'cache_nsa_kv': nrm(ks[2], (DEPTH, n_pool, PAGE_SIZE, 4, NSA_KV, NSA_HD)),
        'cache_mla': nrm(ks[3], (DEPTH, n_pool, PAGE_SIZE, LATENT)),
        'cache_win_kv': nrm(ks[4], (DEPTH, DEC_BATCH, wb, 2, NSA_KV, NSA_HD)),
        'cache_mem_kv': nrm(ks[5], (DEPTH, DEC_BATCH, N_MEM, 2, MEM_HEADS, MEM_HD)),
        'page_table': page_table,
        'mem_prompt': nrm(ks[7], (BATCH, N_MEM, D_MODEL)),
        'g_pre': gain(ks[8], (DEPTH, D_MODEL)),
        'w_in': nrm(ks[9], (DEPTH, D_MODEL, D_IN), D_MODEL ** -0.5),
        'nsa_w_cmp_k': nrm(ks[10], (DEPTH, NSA_BLK, NSA_HD, NSA_HD), (NSA_BLK * NSA_HD) ** -0.5),
        'nsa_w_cmp_v': nrm(ks[11], (DEPTH, NSA_BLK, NSA_HD, NSA_HD), (NSA_BLK * NSA_HD) ** -0.5),
        'mla_g_q': gain(ks[12], (DEPTH, MLA_QLORA)),
        'mla_w_uq': nrm(ks[13], (DEPTH, MLA_QLORA, MLA_HEADS * (MLA_NOPE + MLA_ROPE)), MLA_QLORA ** -0.5),
        'mla_g_kv': gain(ks[14], (DEPTH, MLA_KVLORA)),
        'mla_w_uk': nrm(ks[15], (DEPTH, MLA_HEADS, MLA_NOPE, MLA_KVLORA), MLA_KVLORA ** -0.5),
        'mla_w_uv': nrm(ks[16], (DEPTH, MLA_HEADS, MLA_KVLORA, MLA_V), MLA_KVLORA ** -0.5),
        'mem_g': gain(ks[17], (DEPTH, D_MODEL)),
        'mem_w_kv': nrm(ks[18], (DEPTH, D_MODEL, 2 * MEM_W), D_MODEL ** -0.5),
        'w_br_nsa': nrm(ks[19], (DEPTH, NSA_W, D_MODEL), NSA_W ** -0.5),
        'w_br_mla': nrm(ks[20], (DEPTH, MLA_W, D_MODEL), MLA_W ** -0.5),
        'w_br_mem': nrm(ks[21], (DEPTH, MEM_W, D_MODEL), MEM_W ** -0.5),
        'w_out': nrm(ks[22], (DEPTH, D_MODEL, D_MODEL), D_MODEL ** -0.5),
        'g_post': gain(ks[23], (DEPTH, D_MODEL)),
    }


def reference(x_prompt, x_sample, cache_nsa_kv, cache_mla, cache_win_kv, cache_mem_kv, page_table, mem_prompt,
              g_pre, w_in, nsa_w_cmp_k, nsa_w_cmp_v, mla_g_q, mla_w_uq, mla_g_kv, mla_w_uk, mla_w_uv,
              mem_g, mem_w_kv, w_br_nsa, w_br_mla, w_br_mem, w_out, g_post):
    bp, tp, _ = x_prompt.shape
    ts = x_sample.shape[1]
    n_mem = mem_prompt.shape[1]
    past_len = page_table.shape[1] * PAGE_SIZE
    pos_p = jnp.arange(tp)
    pos_s = past_len + jnp.arange(ts)
    n_new_blk = -(-ts // NSA_BLK)
    blk_pad = ((0, 0), (0, n_new_blk * NSA_BLK - ts), (0, 0), (0, 0))
    xp, xs = x_prompt, x_sample
    nsa_p, nsa_s, mla_p, mla_s, win_p, win_s, memkv_p = [], [], [], [], [], [], []
    for l in range(DEPTH):
        fp = front(xp, pos_p, g_pre[l], w_in[l], mla_g_q[l], mla_w_uq[l], mla_g_kv[l], mla_w_uk[l])
        rows = fp['nsa_rows']
        kc = nsa_compress(rows[:, :, 0], nsa_w_cmp_k[l])
        vc = nsa_compress(rows[:, :, 1], nsa_w_cmp_v[l])
        o_c, sel = nsa_cmp_select(fp['q'], kc, vc, pos_p)
        o_s = nsa_select_prompt(fp['q'], rows[:, :, 2], rows[:, :, 3], sel, pos_p)
        o_w = nsa_window_prompt(fp['q'], fp['kw'], fp['vw'])
        o_nsa = nsa_merge(fp['bg'], o_c, o_s, o_w)
        o_mla = mla_prompt(fp['q_cat'], fp['latent'], mla_w_uv[l])
        mem_kv = (rmsnorm(mem_prompt, mem_g[l]) @ mem_w_kv[l]).reshape(bp, n_mem, 2, MEM_HEADS, MEM_HD)
        o_mem = mem_attend(fp['q_mem'], mem_kv)
        xp = back(xp, fp, o_nsa, o_mla, o_mem, w_br_nsa[l], w_br_mla[l], w_br_mem[l], w_out[l], g_post[l])
        nsa_p.append(rows)
        mla_p.append(fp['latent'])
        win_p.append(jnp.stack([fp['kw'], fp['vw']], axis=2)[:, tp - min(NSA_WINDOW, tp):])
        memkv_p.append(mem_kv)
        fs = front(xs, pos_s, g_pre[l], w_in[l], mla_g_q[l], mla_w_uq[l], mla_g_kv[l], mla_w_uk[l])
        rows_s = fs['nsa_rows']
        kcp, vcp = nsa_cmp_past(cache_nsa_kv, l, page_table, nsa_w_cmp_k[l], nsa_w_cmp_v[l])
        kcs = jnp.concatenate([kcp, nsa_compress(jnp.pad(rows_s[:, :, 0], blk_pad), nsa_w_cmp_k[l])], axis=1)
        vcs = jnp.concatenate([vcp, nsa_compress(jnp.pad(rows_s[:, :, 1], blk_pad), nsa_w_cmp_v[l])], axis=1)
        o_c_s, sel_s = nsa_cmp_select(fs['q'], kcs, vcs, pos_s)
        o_s_s = nsa_select_sample(fs['q'], rows_s[:, :, 2], rows_s[:, :, 3], sel_s, cache_nsa_kv, l, page_table)
        o_w_s, win_new = nsa_window_sample(fs['q'], fs['kw'], fs['vw'], cache_win_kv[l], past_len)
        o_nsa_s = nsa_merge(fs['bg'], o_c_s, o_s_s, o_w_s)
        o_mla_s = mla_sample(fs['q_cat'], fs['latent'], cache_mla, l, page_table, mla_w_uv[l])
        o_mem_s = mem_attend(fs['q_mem'], cache_mem_kv[l])
        xs = back(xs, fs, o_nsa_s, o_mla_s, o_mem_s, w_br_nsa[l], w_br_mla[l], w_br_mem[l], w_out[l], g_post[l])
        nsa_s.append(rows_s)
        mla_s.append(fs['latent'])
        win_s.append(win_new)
    y_prompt, y_sample = xp, xs
    nsa_kv_prompt = jnp.stack(nsa_p, axis=0)
    nsa_kv_sample = jnp.stack(nsa_s, axis=0)
    mla_prompt_rows = jnp.stack(mla_p, axis=0)
    mla_sample_rows = jnp.stack(mla_s, axis=0)
    win_kv_prompt = jnp.stack(win_p, axis=0)
    win_kv_sample = jnp.stack(win_s, axis=0)
    mem_kv_prompt = jnp.stack(memkv_p, axis=0)
    return (y_prompt, y_sample, nsa_kv_prompt, nsa_kv_sample, mla_prompt_rows, mla_sample_rows,
            win_kv_prompt, win_kv_sample, mem_kv_prompt)
```

```python
import functools

import numpy as np
import jax
import jax.numpy as jnp
from jax import lax
from jax.experimental import pallas as pl
from jax.experimental.pallas import tpu as pltpu

F32 = jnp.float32
BF16 = jnp.bfloat16

D_MODEL = 1024
PAGE_SIZE = 128
EPS = 1e-6
ROPE_THETA = 500000.0
NSA_HEADS = 8
NSA_KV = 2
NSA_REP = NSA_HEADS // NSA_KV
NSA_HD = 64
NSA_ROT = NSA_HD // 4
NSA_BLK = 64
NSA_TOPK = 16
NSA_WINDOW = 512
NSA_SCALE = NSA_HD ** -0.5
MLA_HEADS = 8
MLA_QLORA = 384
MLA_KVLORA = 256
MLA_NOPE = 64
MLA_ROPE = 32
MLA_V = 64
MLA_SCALE = (MLA_NOPE + MLA_ROPE) ** -0.5
LATENT = MLA_KVLORA + MLA_ROPE
MEM_HEADS = 4
MEM_HD = 128
MEM_SCALE = MEM_HD ** -0.5
NSA_W = NSA_HEADS * NSA_HD
MLA_W = MLA_HEADS * MLA_V
MEM_W = MEM_HEADS * MEM_HD
N_BRANCH = 3
NSA_KV_W = 3 * 2 * NSA_KV * NSA_HD
IN_SPLITS = (NSA_W, NSA_KV_W, NSA_HEADS * 3, NSA_W,
             MLA_QLORA, MLA_KVLORA, MLA_ROPE, MLA_W,
             MEM_W, MEM_W, N_BRANCH * D_MODEL)
NEG = -1e30
TINY = 1e-30
SEL_BONUS = 1e3
LANE = 128
VMEM_LIMIT = 56 * 1024 * 1024


def _cparams(sem):
    return pltpu.CompilerParams(dimension_semantics=sem, vmem_limit_bytes=VMEM_LIMIT)


def _dot(a, b):
    return jnp.dot(a, b, preferred_element_type=F32)


def _dot_nt(a, b):
    return lax.dot_general(a, b, (((1,), (1,)), ((), ())), preferred_element_type=F32)


def _rms_scale(x):
    return lax.rsqrt(jnp.mean(x * x, axis=-1, keepdims=True) + EPS)


def _softmax_rows(s, mask):
    s = jnp.where(mask, s, NEG)
    p = jnp.where(mask, jnp.exp(s - jnp.max(s, axis=-1, keepdims=True)), 0.0)
    return p / jnp.maximum(jnp.sum(p, axis=-1, keepdims=True), TINY)


def _softmax_rows2(sa, mask_a, sb, mask_b):
    sa = jnp.where(mask_a, sa, NEG)
    sb = jnp.where(mask_b, sb, NEG)
    m = jnp.maximum(jnp.max(sa, axis=-1, keepdims=True), jnp.max(sb, axis=-1, keepdims=True))
    pa = jnp.where(mask_a, jnp.exp(sa - m), 0.0)
    pb = jnp.where(mask_b, jnp.exp(sb - m), 0.0)
    den = jnp.maximum(jnp.sum(pa, axis=-1, keepdims=True) + jnp.sum(pb, axis=-1, keepdims=True), TINY)
    return pa / den, pb / den


def _rope_lanes(x, cs_ref, group, half):
    w = x.shape[1]
    reps = w // LANE
    cos = jnp.concatenate([cs_ref[0]] * reps, axis=1)
    sin = jnp.concatenate([cs_ref[1]] * reps, axis=1)
    lane = lax.broadcasted_iota(jnp.int32, x.shape, 1)
    up = pltpu.roll(x, w - half, 1)
    down = pltpu.roll(x, half, 1)
    rot = jnp.where(lane % group < half, up, down)
    return x * cos + rot * sin


def _front_kernel(x_ref, gpre_ref, wr_ref, wt_ref, gq_ref, wuq_ref, gkv_ref, wuk_ref,
                  csn_ref, csm_ref, tn_ref, tm_ref,
                  q_ref, rc_ref, bg_ref, qmem_ref, qlat_ref, qrp_ref,
                  rows_t_ref, win_t_ref, kv_t_ref, lat_t_ref, latx_ref):
    x = x_ref[0]
    h = (x * _rms_scale(x) * gpre_ref[...]).astype(BF16)
    yr = _dot(h, wr_ref[...])
    q = _rope_lanes(yr[:, 0:NSA_W], csn_ref, NSA_HD, NSA_ROT // 2)
    q_ref[0] = q.astype(BF16)
    kc = _rope_lanes(yr[:, 512:640], csn_ref, NSA_HD, NSA_ROT // 2)
    rc_ref[0] = jnp.concatenate([kc, yr[:, 640:768]], axis=1).astype(BF16)
    cq = yr[:, 768:1152]
    qmem_ref[0] = yr[:, 1152:1664].astype(BF16)
    bg_ref[0] = jax.nn.sigmoid(yr[:, 1664:1792])
    cqn = (cq * _rms_scale(cq) * gq_ref[...]).astype(BF16)
    qm = _dot(cqn, wuq_ref[...])
    qrp_ref[0] = _rope_lanes(qm[:, 512:768], csm_ref, MLA_ROPE, MLA_ROPE // 2).astype(BF16)
    for hd in range(MLA_HEADS):
        qn = qm[:, hd * MLA_NOPE:(hd + 1) * MLA_NOPE].astype(BF16)
        qlat_ref[0, hd] = _dot(qn, wuk_ref[hd]).astype(BF16)
    yt = _dot_nt(wt_ref[...], h)
    cn, sn = tn_ref[0], tn_ref[1]
    half = NSA_ROT // 2
    pieces = []
    for base in range(0, NSA_KV_W, NSA_HD):
        is_key = (base // (NSA_KV * NSA_HD)) % 2 == 0
        if is_key:
            x1 = yt[base:base + half]
            x2 = yt[base + half:base + 2 * half]
            pieces += [x1 * cn - x2 * sn, x2 * cn + x1 * sn, yt[base + 2 * half:base + NSA_HD]]
        else:
            pieces.append(yt[base:base + NSA_HD])
    kvt = jnp.concatenate(pieces, axis=0)
    rows_t_ref[0] = kvt[0:512]
    win_t_ref[0] = kvt[512:768]
    kv_t_ref[0] = kvt[256:768].astype(BF16)
    ckv = yt[768:1024]
    ckvn = ckv * lax.rsqrt(jnp.mean(ckv * ckv, axis=0, keepdims=True) + EPS) * gkv_ref[...]
    cm, sm = tm_ref[0], tm_ref[1]
    k1 = yt[1024:1040]
    k2 = yt[1040:1056]
    kr = jnp.concatenate([k1 * cm - k2 * sm, k2 * cm + k1 * sm], axis=0)
    lat_t_ref[0] = jnp.concatenate([ckvn, kr], axis=0)
    krb = kr.astype(BF16)
    latx_ref[0] = jnp.concatenate([ckvn.astype(BF16)] + [krb] * MLA_HEADS, axis=0)


def _rope_tables(pos):
    pos = pos.astype(F32)

    def tables(n_rot, group):
        half = n_rot // 2
        inv = ROPE_THETA ** (-jnp.arange(half, dtype=F32) / half)
        ang = pos[:, None] * inv[None, :]
        cos, sin = jnp.cos(ang), jnp.sin(ang)
        t = pos.shape[0]
        pad1 = jnp.ones((t, group - n_rot), F32)
        pad0 = jnp.zeros((t, group - n_rot), F32)
        cos_g = jnp.concatenate([cos, cos, pad1], axis=1)
        sin_g = jnp.concatenate([-sin, sin, pad0], axis=1)
        reps = LANE // group
        row = jnp.stack([jnp.tile(cos_g, (1, reps)), jnp.tile(sin_g, (1, reps))])
        col = jnp.stack([cos.T, sin.T])
        return row, col

    csn, tn = tables(NSA_ROT, NSA_HD)
    csm, tm = tables(MLA_ROPE, MLA_ROPE)
    return csn, csm, tn, tm


def _front_weights(w_in, mla_w_uq):
    o = np.cumsum((0,) + IN_SPLITS)
    w_q, w_kv, w_bg = w_in[:, o[0]:o[1]], w_in[:, o[1]:o[2]], w_in[:, o[2]:o[3]]
    w_cq, w_ckv, w_kr = w_in[:, o[4]:o[5]], w_in[:, o[5]:o[6]], w_in[:, o[6]:o[7]]
    w_qmem = w_in[:, o[8]:o[9]]
    pad = jnp.zeros((D_MODEL, LANE - NSA_HEADS * 3), F32)
    w_row = jnp.concatenate([w_q, w_kv[:, 0:256], w_cq, w_qmem, w_bg, pad], axis=1).astype(BF16)
    w_t = jnp.concatenate([w_kv, w_ckv, w_kr], axis=1).T.astype(BF16)
    wu = mla_w_uq.reshape(MLA_QLORA, MLA_HEADS, MLA_NOPE + MLA_ROPE)
    w_uq = jnp.concatenate([wu[:, :, :MLA_NOPE].reshape(MLA_QLORA, -1),
                            wu[:, :, MLA_NOPE:].reshape(MLA_QLORA, -1)], axis=1).astype(BF16)
    return w_row, w_t, w_uq


def _front(x, pos, g_pre, w_row, w_t, g_q, w_uq, g_kv, w_uk, tm):
    b, t, _ = x.shape
    csn, csm, tn, tmm = _rope_tables(pos)
    nt = t // tm
    tok = lambda w: pl.BlockSpec((1, tm, w), lambda i, j: (i, j, 0))
    tlay = lambda r: pl.BlockSpec((1, r, tm), lambda i, j: (i, 0, j))
    full = lambda a: pl.BlockSpec(a.shape, lambda i, j: (0,) * a.ndim)
    g_pre2, g_q2 = g_pre.reshape(1, -1), g_q.reshape(1, -1)
    g_kv2 = g_kv.reshape(-1, 1)
    w_uk = w_uk.astype(BF16)
    in_specs = [tok(D_MODEL), full(g_pre2), full(w_row), full(w_t), full(g_q2), full(w_uq), full(g_kv2),
                full(w_uk),
                pl.BlockSpec((2, tm, LANE), lambda i, j: (0, j, 0)),
                pl.BlockSpec((2, tm, LANE), lambda i, j: (0, j, 0)),
                pl.BlockSpec((2, NSA_ROT // 2, tm), lambda i, j: (0, 0, j)),
                pl.BlockSpec((2, MLA_ROPE // 2, tm), lambda i, j: (0, 0, j))]
    out_shape = [
        jax.ShapeDtypeStruct((b, t, NSA_W), BF16),
        jax.ShapeDtypeStruct((b, t, 256), BF16),
        jax.ShapeDtypeStruct((b, t, LANE), F32),
        jax.ShapeDtypeStruct((b, t, MEM_W), BF16),
        jax.ShapeDtypeStruct((b, MLA_HEADS, t, MLA_KVLORA), BF16),
        jax.ShapeDtypeStruct((b, t, MLA_HEADS * MLA_ROPE), BF16),
        jax.ShapeDtypeStruct((b, 512, t), F32),
        jax.ShapeDtypeStruct((b, 256, t), F32),
        jax.ShapeDtypeStruct((b, 512, t), BF16),
        jax.ShapeDtypeStruct((b, LATENT, t), F32),
        jax.ShapeDtypeStruct((b, 512, t), BF16),
    ]
    out_specs = [tok(NSA_W), tok(256), tok(LANE), tok(MEM_W),
                 pl.BlockSpec((1, MLA_HEADS, tm, MLA_KVLORA), lambda i, j: (i, 0, j, 0)),
                 tok(MLA_HEADS * MLA_ROPE),
                 tlay(512), tlay(256), tlay(512), tlay(LATENT), tlay(512)]
    return pl.pallas_call(
        _front_kernel, grid=(b, nt), in_specs=in_specs, out_specs=out_specs, out_shape=out_shape,
        compiler_params=_cparams(("parallel", "parallel")), name="front",
    )(x, g_pre2, w_row, w_t, g_q2, w_uq, g_kv2, w_uk, csn, csm, tn, tmm)


def _compress_weights(wk, wv):
    z = jnp.zeros_like(wk)
    rows = [jnp.concatenate(r, axis=2) for r in ([wk, z, z, z], [z, wk, z, z], [z, z, wv, z], [z, z, z, wv])]
    return jnp.concatenate(rows, axis=1).astype(BF16)


def _compress_kernel(x_ref, w_ref, o_ref):
    @pl.when(pl.program_id(0) == 0)
    def _():
        o_ref[...] = jnp.zeros_like(o_ref)

    o_ref[...] += _dot(x_ref[...], w_ref[0])


def _compress(x, w_bd, n_l):
    r = x.shape[0]
    return pl.pallas_call(
        _compress_kernel, grid=(n_l,),
        in_specs=[pl.BlockSpec((r, 256), lambda l: (0, l)),
                  pl.BlockSpec((1, 256, 256), lambda l: (l, 0, 0))],
        out_specs=pl.BlockSpec((r, 256), lambda l: (0, 0)),
        out_shape=jax.ShapeDtypeStruct((r, 256), F32),
        compiler_params=_cparams(("arbitrary",)), name="nsa_compress",
    )(x, w_bd)


def _select_blocks(score, blk, n_sel):
    rank = jnp.zeros(score.shape, jnp.int32)
    for m in range(score.shape[1]):
        sm = score[:, m:m + 1]
        rank += ((sm > score) | ((sm == score) & (m < blk))).astype(jnp.int32)
    return rank < n_sel


def _nsa_prompt_kernel(q_ref, bg_ref, kcvc_ref, kv_ref, o_ref, *, tq, t_len):
    nb = t_len // NSA_BLK
    q0 = pl.program_id(1) * tq
    qpos = q0 + lax.broadcasted_iota(jnp.int32, (tq, 1), 0)
    blk = lax.broadcasted_iota(jnp.int32, (1, nb), 1)
    vis = (blk + 1) * NSA_BLK <= qpos + 1
    cur = qpos // NSA_BLK
    valid = blk <= cur
    forced = (blk == 0) | (blk == cur) | (blk == cur - 1)
    key = lax.broadcasted_iota(jnp.int32, (1, t_len), 1)
    expand = (lax.broadcasted_iota(jnp.int32, (nb, t_len), 1) // NSA_BLK
              == lax.broadcasted_iota(jnp.int32, (nb, t_len), 0)).astype(BF16)
    wlen = min(NSA_WINDOW + tq, t_len)
    wstart = pl.multiple_of(jnp.maximum(q0 - NSA_WINDOW, 0), LANE)
    wdist = qpos - (wstart + lax.broadcasted_iota(jnp.int32, (1, wlen), 1))
    wmask = (wdist >= 0) & (wdist < NSA_WINDOW)
    q = q_ref[0]
    gates = bg_ref[0]
    kcvc = kcvc_ref[0].astype(BF16)
    outs = []
    for g in range(NSA_KV):
        kc = kcvc[:, g * NSA_HD:(g + 1) * NSA_HD]
        vc = kcvc[:, 128 + g * NSA_HD:128 + (g + 1) * NSA_HD]
        qs = [q[:, (g * NSA_REP + r) * NSA_HD:(g * NSA_REP + r + 1) * NSA_HD] for r in range(NSA_REP)]
        o_cmp = []
        psum = jnp.zeros((tq, nb), F32)
        for r in range(NSA_REP):
            p = _softmax_rows(_dot_nt(qs[r], kc) * NSA_SCALE, vis)
            o_cmp.append(_dot(p.astype(BF16), vc))
            psum = psum + p
        score = jnp.where(valid, psum + jnp.where(forced, SEL_BONUS, 0.0), -jnp.inf)
        sel = _select_blocks(score, blk, min(NSA_TOPK, nb))
        smask = (_dot(sel.astype(BF16), expand) > 0.5) & (key <= qpos)
        ks = kv_ref[0, g * NSA_HD:(g + 1) * NSA_HD, :]
        vs = kv_ref[0, 128 + g * NSA_HD:128 + (g + 1) * NSA_HD, :]
        kw = kv_ref[0, 256 + g * NSA_HD:256 + (g + 1) * NSA_HD, pl.ds(wstart, wlen)]
        vw = kv_ref[0, 384 + g * NSA_HD:384 + (g + 1) * NSA_HD, pl.ds(wstart, wlen)]
        for r in range(NSA_REP):
            h = g * NSA_REP + r
            p = _softmax_rows(_dot(qs[r], ks) * NSA_SCALE, smask)
            o_slc = _dot_nt(p.astype(BF16), vs)
            p = _softmax_rows(_dot(qs[r], kw) * NSA_SCALE, wmask)
            o_win = _dot_nt(p.astype(BF16), vw)
            outs.append(gates[:, 3 * h:3 * h + 1] * o_cmp[r] + gates[:, 3 * h + 1:3 * h + 2] * o_slc
                        + gates[:, 3 * h + 2:3 * h + 3] * o_win)
    o_ref[0] = jnp.concatenate(outs, axis=1)


def _nsa_prompt(q, bg, kcvc, kv_t, tq):
    b, t, _ = q.shape
    nb = t // NSA_BLK
    tok = lambda w: pl.BlockSpec((1, tq, w), lambda i, j: (i, j, 0))
    return pl.pallas_call(
        functools.partial(_nsa_prompt_kernel, tq=tq, t_len=t), grid=(b, t // tq),
        in_specs=[tok(NSA_W), tok(LANE),
                  pl.BlockSpec((1, nb, 256), lambda i, j: (i, 0, 0)),
                  pl.BlockSpec((1, 512, t), lambda i, j: (i, 0, 0))],
        out_specs=tok(NSA_W),
        out_shape=jax.ShapeDtypeStruct((b, t, NSA_W), F32),
        compiler_params=_cparams(("parallel", "parallel")), name="nsa_prompt",
    )(q, bg, kcvc, kv_t)


def _stack_rope_queries(qrp):
    lane_head = lax.broadcasted_iota(jnp.int32, qrp.shape, 1) // MLA_ROPE
    return jnp.concatenate([jnp.where(lane_head == h, qrp, jnp.zeros_like(qrp)) for h in range(MLA_HEADS)],
                           axis=0)


def _mla_prompt_kernel(qlat_ref, qrp_ref, latx_ref, wuv_ref, o_ref, *, tq, t_len):
    q0 = pl.program_id(1) * tq
    q1 = qlat_ref[0].reshape(MLA_HEADS * tq, MLA_KVLORA)
    q2 = _stack_rope_queries(qrp_ref[0])
    lat = latx_ref[0, 0:MLA_KVLORA, :]
    s = (_dot(q1, lat) + _dot(q2, latx_ref[0, MLA_KVLORA:, :])) * MLA_SCALE
    row = lax.broadcasted_iota(jnp.int32, (MLA_HEADS * tq, 1), 0)
    qpos = q0 + row % tq
    key = lax.broadcasted_iota(jnp.int32, (1, t_len), 1)
    p = _softmax_rows(s, key <= qpos)
    o_lat = _dot_nt(p.astype(BF16), lat).astype(BF16)
    o_ref[0] = jnp.concatenate([_dot(o_lat[h * tq:(h + 1) * tq], wuv_ref[h]) for h in range(MLA_HEADS)], axis=1)


def _mla_prompt(qlat, qrp, latx, w_uv, tq):
    b, _, t, _ = qlat.shape
    return pl.pallas_call(
        functools.partial(_mla_prompt_kernel, tq=tq, t_len=t), grid=(b, t // tq),
        in_specs=[pl.BlockSpec((1, MLA_HEADS, tq, MLA_KVLORA), lambda i, j: (i, 0, j, 0)),
                  pl.BlockSpec((1, tq, 256), lambda i, j: (i, j, 0)),
                  pl.BlockSpec((1, 512, t), lambda i, j: (i, 0, 0)),
                  pl.BlockSpec(w_uv.shape, lambda i, j: (0, 0, 0))],
        out_specs=pl.BlockSpec((1, tq, MLA_W), lambda i, j: (i, j, 0)),
        out_shape=jax.ShapeDtypeStruct((b, t, MLA_W), F32),
        compiler_params=_cparams(("parallel", "parallel")), name="mla_prompt",
    )(qlat, qrp, latx, w_uv)


def _mem_kv_kernel(m_ref, g_ref, w_ref, o_ref):
    m = m_ref[...]
    o_ref[...] = _dot((m * _rms_scale(m) * g_ref[...]).astype(BF16), w_ref[...])


def _mem_kv(mem, g, w, tm):
    n = mem.shape[0]
    g2 = g.reshape(1, -1)
    return pl.pallas_call(
        _mem_kv_kernel, grid=(n // tm,),
        in_specs=[pl.BlockSpec((tm, D_MODEL), lambda i: (i, 0)),
                  pl.BlockSpec(g2.shape, lambda i: (0, 0)),
                  pl.BlockSpec(w.shape, lambda i: (0, 0))],
        out_specs=pl.BlockSpec((tm, 2 * MEM_W), lambda i: (i, 0)),
        out_shape=jax.ShapeDtypeStruct((n, 2 * MEM_W), F32),
        compiler_params=_cparams(("parallel",)), name="mem_kv",
    )(mem, g2, w)


def _mem_attend_kernel(q_ref, kv_ref, o_ref):
    q = q_ref[0]
    outs = []
    for h in range(MEM_HEADS):
        k = kv_ref[0, :, h * MEM_HD:(h + 1) * MEM_HD].astype(BF16)
        v = kv_ref[0, :, MEM_W + h * MEM_HD:MEM_W + (h + 1) * MEM_HD].astype(BF16)
        s = _dot_nt(q[:, h * MEM_HD:(h + 1) * MEM_HD], k) * MEM_SCALE
        e = jnp.exp(s - jnp.max(s, axis=-1, keepdims=True))
        p = e / jnp.sum(e, axis=-1, keepdims=True)
        outs.append(_dot(p.astype(BF16), v))
    o_ref[0] = jnp.concatenate(outs, axis=1)


def _mem_attend(q, kv, tq):
    b, t, _ = q.shape
    return pl.pallas_call(
        _mem_attend_kernel, grid=(b, t // tq),
        in_specs=[pl.BlockSpec((1, tq, MEM_W), lambda i, j: (i, j, 0)),
                  pl.BlockSpec((1,) + kv.shape[1:], lambda i, j: (i, 0, 0))],
        out_specs=pl.BlockSpec((1, tq, MEM_W), lambda i, j: (i, j, 0)),
        out_shape=jax.ShapeDtypeStruct((b, t, MEM_W), F32),
        compiler_params=_cparams(("parallel", "parallel")), name="mem_attend",
    )(q, kv)


def _back_kernel(x_ref, on_ref, om_ref, oe_ref, gpre_ref, wz_ref, wg_ref, wbr_ref, wout_ref, gpost_ref, y_ref):
    x = x_ref[...]
    h = (x * _rms_scale(x) * gpre_ref[...]).astype(BF16)
    y = jnp.zeros(x.shape, F32)
    for j, o_ref in enumerate((on_ref, om_ref, oe_ref)):
        z = _dot(h, wz_ref[j])
        u = (o_ref[...] * (z * jax.nn.sigmoid(z))).astype(BF16)
        gate = jax.nn.sigmoid(_dot(h, wg_ref[j]))
        y = y + gate * _dot(u, wbr_ref[j])
    v = _dot(y.astype(BF16), wout_ref[...])
    y_ref[...] = x + v * _rms_scale(v) * gpost_ref[...]


def _back_weights(w_in, w_br_nsa, w_br_mla, w_br_mem):
    o = np.cumsum((0,) + IN_SPLITS)
    w_z = jnp.stack([w_in[:, o[3]:o[4]], w_in[:, o[7]:o[8]], w_in[:, o[9]:o[10]]]).astype(BF16)
    w_g = jnp.stack([w_in[:, o[10] + j * D_MODEL:o[10] + (j + 1) * D_MODEL] for j in range(N_BRANCH)]).astype(BF16)
    w_br = jnp.stack([w_br_nsa, w_br_mla, w_br_mem]).astype(BF16)
    return w_z, w_g, w_br


def _back(x, o_nsa, o_mla, o_mem, g_pre, w_z, w_g, w_br, w_out, g_post, tm):
    n = x.shape[0]
    g_pre2, g_post2 = g_pre.reshape(1, -1), g_post.reshape(1, -1)
    full = lambda a: pl.BlockSpec(a.shape, lambda i: (0,) * a.ndim)
    tok = lambda w: pl.BlockSpec((tm, w), lambda i: (i, 0))
    return pl.pallas_call(
        _back_kernel, grid=(n // tm,),
        in_specs=[tok(D_MODEL), tok(512), tok(512), tok(512), full(g_pre2), full(w_z), full(w_g), full(w_br),
                  full(w_out), full(g_post2)],
        out_specs=tok(D_MODEL),
        out_shape=jax.ShapeDtypeStruct((n, D_MODEL), F32),
        compiler_params=_cparams(("parallel",)), name="back",
    )(x, o_nsa, o_mla, o_mem, g_pre2, w_z, w_g, w_br, w_out, g_post2)


def _pipelined_gather(step, n_steps, n_copies, make_copies):
    slot = step % 2

    def run(s, sl, start):
        def body(i, carry):
            for cp in make_copies(s, i, sl):
                if start:
                    cp.start()
                else:
                    cp.wait()
            return carry
        lax.fori_loop(0, n_copies, body, 0)

    @pl.when(step == 0)
    def _():
        run(step, slot, True)

    @pl.when(step + 1 < n_steps)
    def _():
        run(step + 1, 1 - slot, True)

    run(step, slot, False)
    return slot


def _past_compress_weights(wk, wv):
    def one(w):
        m = jnp.einsum('jk,lde->djlke', jnp.eye(2, dtype=F32), w).reshape(NSA_HD, 128, 128)
        return m.reshape(NSA_HD // 2, 256, 128)
    return jnp.stack([one(wk), one(wv)]).astype(BF16)


def _past_compress_kernel(pt_ref, pool_ref, w_ref, o_ref, buf, sem, *, n_pages):
    chunk = buf.shape[2]
    steps_per_b = n_pages // chunk
    step = pl.program_id(0) * steps_per_b + pl.program_id(1)

    def page_copy(s, i, slot):
        page = pt_ref[s // steps_per_b, (s % steps_per_b) * chunk + i]
        return (pltpu.make_async_copy(pool_ref.at[page, pl.ds(0, 256), :], buf.at[slot, :, i, :], sem.at[slot]),)

    slot = _pipelined_gather(step, pl.num_programs(0) * steps_per_b, chunk, page_copy)

    for kv in range(2):
        def body(dp, acc):
            parts = []
            for g in range(NSA_KV):
                r0 = kv * 128 + g * NSA_HD + 2 * dp
                parts.append(jnp.concatenate([buf[slot, r0], buf[slot, r0 + 1]], axis=1))
            a = jnp.concatenate(parts, axis=0).astype(BF16)
            return acc + _dot(a, w_ref[kv, dp])
        acc = lax.fori_loop(0, NSA_HD // 2, body, jnp.zeros((NSA_KV * chunk, 128), F32))
        for g in range(NSA_KV):
            o_ref[0, kv, g] = acc[g * chunk:(g + 1) * chunk]


def _past_compress(page_table, pool_t, w2, chunk):
    bs, n_pages = page_table.shape
    grid_spec = pltpu.PrefetchScalarGridSpec(
        num_scalar_prefetch=1, grid=(bs, n_pages // chunk),
        in_specs=[pl.BlockSpec(memory_space=pl.ANY),
                  pl.BlockSpec(w2.shape, lambda b, c, pt: (0, 0, 0, 0))],
        out_specs=pl.BlockSpec((1, 2, NSA_KV, chunk, 128), lambda b, c, pt: (b, 0, 0, c, 0)),
        scratch_shapes=[pltpu.VMEM((2, 256, chunk, PAGE_SIZE), F32), pltpu.SemaphoreType.DMA((2,))])
    out = pl.pallas_call(
        functools.partial(_past_compress_kernel, n_pages=n_pages), grid_spec=grid_spec,
        out_shape=jax.ShapeDtypeStruct((bs, 2, NSA_KV, n_pages, 128), F32),
        compiler_params=_cparams(("arbitrary", "arbitrary")), name="past_compress",
    )(page_table, pool_t, w2)
    return out.reshape(bs, 2, NSA_KV, n_pages * (PAGE_SIZE // NSA_BLK), NSA_HD)


def _cmp_sample_kernel(q_ref, kcvc_ref, oc_ref, sel_ref, sc_ref, *, ts, past_len, nb):
    nbp = kcvc_ref.shape[3]
    rows = NSA_REP * ts
    t_row = lax.broadcasted_iota(jnp.int32, (rows, 1), 0) % ts
    blk = lax.broadcasted_iota(jnp.int32, (1, nbp), 1)
    vis = ((blk + 1) * NSA_BLK <= past_len + t_row + 1) & (blk < nb)
    for g in range(NSA_KV):
        kc = kcvc_ref[0, 0, g].astype(BF16)
        vc = kcvc_ref[0, 1, g].astype(BF16)
        p = _softmax_rows(_dot_nt(q_ref[0, g], kc) * NSA_SCALE, vis)
        oc_ref[0, g] = _dot(p.astype(BF16), vc)
        psum = p[0:ts]
        for r in range(1, NSA_REP):
            psum = psum + p[r * ts:(r + 1) * ts]
        sc_ref[g * ts:(g + 1) * ts, :] = psum
    n_rows = NSA_KV * ts
    t_sel = lax.broadcasted_iota(jnp.int32, (n_rows, 1), 0) % ts
    cur = (past_len + t_sel) // NSA_BLK
    valid = (blk <= cur) & (blk < nb)
    forced = (blk == 0) | (blk == cur) | (blk == cur - 1)
    score = jnp.where(valid, sc_ref[...] + jnp.where(forced, SEL_BONUS, 0.0), -jnp.inf)
    blk_f = blk.astype(F32)
    taken = jnp.broadcast_to(blk >= nb, score.shape)
    col = lax.broadcasted_iota(jnp.int32, (n_rows, LANE), 1)
    sel = jnp.zeros((n_rows, LANE), F32)
    for k in range(min(NSA_TOPK, nb)):
        best = jnp.max(jnp.where(taken, -jnp.inf, score), axis=1, keepdims=True)
        idx = jnp.min(jnp.where((~taken) & (score == best), blk_f, float(nbp)), axis=1, keepdims=True)
        sel = jnp.where(col == k, idx, sel)
        taken = taken | (blk_f == idx)
    sel_ref[0] = sel.astype(jnp.int32)


def _cmp_sample(qg, kcvc, ts, past_len, nb):
    bs = qg.shape[0]
    rows = NSA_REP * ts
    nbp = kcvc.shape[3]
    return pl.pallas_call(
        functools.partial(_cmp_sample_kernel, ts=ts, past_len=past_len, nb=nb), grid=(bs,),
        in_specs=[pl.BlockSpec((1, NSA_KV, rows, NSA_HD), lambda b: (b, 0, 0, 0)),
                  pl.BlockSpec((1, 2, NSA_KV, nbp, NSA_HD), lambda b: (b, 0, 0, 0, 0))],
        out_specs=[pl.BlockSpec((1, NSA_KV, rows, NSA_HD), lambda b: (b, 0, 0, 0)),
                   pl.BlockSpec((1, NSA_KV * ts, LANE), lambda b: (b, 0, 0))],
        out_shape=[jax.ShapeDtypeStruct((bs, NSA_KV, rows, NSA_HD), F32),
                   jax.ShapeDtypeStruct((bs, NSA_KV * ts, LANE), jnp.int32)],
        scratch_shapes=[pltpu.VMEM((NSA_KV * ts, nbp), F32)],
        compiler_params=_cparams(("parallel",)), name="cmp_sample",
    )(qg, kcvc)


def _slc_sample_kernel(pt_ref, sel_ref, q_ref, gate_ref, oc_ref, pool_ref, kvn_ref, win_ref, winn_ref,
                       o_ref, wout_ref, kbuf, vbuf, sem, *, ts, past_len, n_sel):
    b = pl.program_id(0)
    nbs = pl.num_programs(0)
    bpp = PAGE_SIZE // NSA_BLK
    nbp = pt_ref.shape[1] * bpp
    n_tg = NSA_KV * ts
    rows = NSA_REP * ts
    per_b = LANE // ts

    def copies(bb, i, slot):
        gt, k = i // n_sel, i % n_sel
        g = gt // ts
        s = jnp.minimum(sel_ref[bb, i], nbp - 1)
        page = pt_ref[bb, s // bpp]
        dst = pl.ds(pl.multiple_of(k * PAGE_SIZE, PAGE_SIZE), PAGE_SIZE)
        return (pltpu.make_async_copy(pool_ref.at[page, pl.ds(pl.multiple_of(256 + g * NSA_HD, NSA_HD), NSA_HD), :],
                                      kbuf.at[slot, gt, :, dst], sem.at[slot]),
                pltpu.make_async_copy(pool_ref.at[page, pl.ds(pl.multiple_of(384 + g * NSA_HD, NSA_HD), NSA_HD), :],
                                      vbuf.at[slot, gt, :, dst], sem.at[slot]))

    slot = _pipelined_gather(b, nbs, n_tg * n_sel, copies)

    t_row = lax.broadcasted_iota(jnp.int32, (rows, 1), 0) % ts
    lane_p = lax.broadcasted_iota(jnp.int32, (1, n_sel * PAGE_SIZE), 1)
    lane_n = lax.broadcasted_iota(jnp.int32, (1, LANE), 1)
    mine = lane_n // ts == b % per_b
    t_key = lane_n % ts
    wb = win_ref.shape[2]
    lane_w = lax.broadcasted_iota(jnp.int32, (1, wb), 1)
    wdist = t_row + wb - lane_w
    wmask = (wdist >= 0) & (wdist < NSA_WINDOW)
    wdist_n = t_row - t_key
    wmask_n = mine & (wdist_n >= 0) & (wdist_n < NSA_WINDOW)
    for g in range(NSA_KV):
        q = q_ref[0, g]
        gates = gate_ref[0, g]
        kn = kvn_ref[0, g * NSA_HD:(g + 1) * NSA_HD, :]
        vn = kvn_ref[0, 128 + g * NSA_HD:128 + (g + 1) * NSA_HD, :]
        s_new = _dot(q, kn) * NSA_SCALE
        o_slc = jnp.zeros((rows, NSA_HD), F32)
        for t in range(ts):
            selv = jnp.zeros((1, n_sel * PAGE_SIZE), jnp.int32)
            chosen = jnp.zeros((1, LANE), jnp.bool_)
            for k in range(n_sel):
                s_k = sel_ref[b, (g * ts + t) * n_sel + k]
                selv = jnp.where(lane_p // PAGE_SIZE == k, s_k, selv)
                chosen = chosen | ((past_len + t_key) // NSA_BLK == s_k)
            pmask = (selv < nbp) & ((lane_p % PAGE_SIZE) // NSA_BLK == selv % bpp) & (t_row == t)
            nmask = mine & chosen & (t_key <= t_row) & (t_row == t)
            kt = kbuf[slot, g * ts + t].astype(BF16)
            vt = vbuf[slot, g * ts + t].astype(BF16)
            pa, pb = _softmax_rows2(_dot(q, kt) * NSA_SCALE, pmask, s_new, nmask)
            o_slc = o_slc + _dot_nt(pa.astype(BF16), vt) + _dot_nt(pb.astype(BF16), vn)
        kw = win_ref[0, g * NSA_HD:(g + 1) * NSA_HD, :].astype(BF16)
        vw = win_ref[0, 128 + g * NSA_HD:128 + (g + 1) * NSA_HD, :].astype(BF16)
        kwn = kvn_ref[0, 256 + g * NSA_HD:256 + (g + 1) * NSA_HD, :]
        vwn = kvn_ref[0, 384 + g * NSA_HD:384 + (g + 1) * NSA_HD, :]
        pa, pb = _softmax_rows2(_dot(q, kw) * NSA_SCALE, wmask, _dot(q, kwn) * NSA_SCALE, wmask_n)
        o_win = _dot_nt(pa.astype(BF16), vw) + _dot_nt(pb.astype(BF16), vwn)
        o_ref[0, g] = gates[:, 0:1] * oc_ref[0, g] + gates[:, 1:2] * o_slc + gates[:, 2:3] * o_win
    shifted = pltpu.roll(win_ref[0], wb - ts, 1)
    wout_ref[0] = shifted
    new_cols = pltpu.roll(winn_ref[0], (LANE - ts - (b % per_b) * ts) % LANE, 1)
    wout_ref[0, :, wb - LANE:] = jnp.where(lane_n >= LANE - ts, new_cols, shifted[:, wb - LANE:])


def _slc_sample(page_table, sel, qg, gates, o_cmp, pool_t, kvn_t, win_t, winn_t, ts, past_len):
    bs = qg.shape[0]
    rows = NSA_REP * ts
    n_sel = sel.shape[1] // (NSA_KV * ts)
    wb = win_t.shape[2]
    per_b = LANE // ts
    qspec = lambda w: pl.BlockSpec((1, NSA_KV, rows, w), lambda b, pt, sl: (b, 0, 0, 0))
    grid_spec = pltpu.PrefetchScalarGridSpec(
        num_scalar_prefetch=2, grid=(bs,),
        in_specs=[qspec(NSA_HD), qspec(3), qspec(NSA_HD),
                  pl.BlockSpec(memory_space=pl.ANY),
                  pl.BlockSpec((1, 512, LANE), lambda b, pt, sl: (0, 0, b // per_b)),
                  pl.BlockSpec((1, 256, wb), lambda b, pt, sl: (b, 0, 0)),
                  pl.BlockSpec((1, 256, LANE), lambda b, pt, sl: (0, 0, b // per_b))],
        out_specs=[qspec(NSA_HD), pl.BlockSpec((1, 256, wb), lambda b, pt, sl: (b, 0, 0))],
        scratch_shapes=[pltpu.VMEM((2, NSA_KV * ts, NSA_HD, n_sel * PAGE_SIZE), F32),
                        pltpu.VMEM((2, NSA_KV * ts, NSA_HD, n_sel * PAGE_SIZE), F32),
                        pltpu.SemaphoreType.DMA((2,))])
    return pl.pallas_call(
        functools.partial(_slc_sample_kernel, ts=ts, past_len=past_len, n_sel=n_sel), grid_spec=grid_spec,
        out_shape=[jax.ShapeDtypeStruct((bs, NSA_KV, rows, NSA_HD), F32),
                   jax.ShapeDtypeStruct((bs, 256, wb), F32)],
        compiler_params=_cparams(("arbitrary",)), name="slc_sample",
    )(page_table, sel, qg, gates, o_cmp, pool_t, kvn_t, win_t, winn_t)


def _mla_sample_kernel(pt_ref, q1_ref, q2_ref, pool_ref, latn_ref, wuv_ref, o_ref,
                       buf, kb, s_buf, p_buf, m_ref, l_ref, acc_ref, sem, *, ts, n_pages):
    n_chunk = buf.shape[1]
    steps_per_b = n_pages // n_chunk
    b, c = pl.program_id(0), pl.program_id(1)
    step = b * steps_per_b + c
    rows = MLA_HEADS * ts

    def page_copy(s, i, slot):
        page = pt_ref[s // steps_per_b, (s % steps_per_b) * n_chunk + i]
        return (pltpu.make_async_copy(pool_ref.at[page], buf.at[slot, i], sem.at[slot]),)

    slot = _pipelined_gather(step, pl.num_programs(0) * steps_per_b, n_chunk, page_copy)

    @pl.when(c == 0)
    def _():
        m_ref[...] = jnp.full(m_ref.shape, NEG, F32)
        l_ref[...] = jnp.zeros(l_ref.shape, F32)
        acc_ref[...] = jnp.zeros(acc_ref.shape, F32)

    q1 = q1_ref[0]
    q2 = q2_ref[0]

    def update(s, pv):
        m_old = m_ref[...]
        m_new = jnp.maximum(m_old, jnp.max(s, axis=-1, keepdims=True))
        alpha = jnp.exp(m_old - m_new)
        p = jnp.exp(s - m_new)
        m_ref[...] = m_new
        l_ref[...] = l_ref[...] * alpha + jnp.sum(p, axis=-1, keepdims=True)
        acc_ref[...] = acc_ref[...] * alpha + pv(p.astype(BF16))

    def score_page(i, carry):
        page = buf[slot, i].astype(BF16)
        kb[i] = page
        kr8 = jnp.concatenate([page[MLA_KVLORA:]] * MLA_HEADS, axis=0)
        s = (_dot(q1, page[0:MLA_KVLORA]) + _dot(q2, kr8)) * MLA_SCALE
        s_buf[:, pl.ds(pl.multiple_of(i * PAGE_SIZE, PAGE_SIZE), PAGE_SIZE)] = s
        return carry

    lax.fori_loop(0, n_chunk, score_page, 0)

    def pv_pages(p):
        p_buf[...] = p

        def body(i, acc):
            pi = p_buf[:, pl.ds(pl.multiple_of(i * PAGE_SIZE, PAGE_SIZE), PAGE_SIZE)]
            return acc + _dot_nt(pi, kb[i, 0:MLA_KVLORA, :])
        return lax.fori_loop(0, n_chunk, body, jnp.zeros((rows, MLA_KVLORA), F32))

    update(s_buf[...], pv_pages)

    @pl.when(c == steps_per_b - 1)
    def _():
        latn = latn_ref[0]
        per_b = LANE // ts
        lane_n = lax.broadcasted_iota(jnp.int32, (1, LANE), 1)
        t_row = lax.broadcasted_iota(jnp.int32, (rows, 1), 0) // MLA_HEADS
        mask = (lane_n // ts == b % per_b) & (lane_n % ts <= t_row)
        s = (_dot(q1, latn[0:MLA_KVLORA]) + _dot(q2, latn[MLA_KVLORA:])) * MLA_SCALE
        update(jnp.where(mask, s, NEG), lambda p: _dot_nt(p, latn[0:MLA_KVLORA]))
        o_lat = (acc_ref[...] / l_ref[...]).astype(BF16)
        full = _dot(o_lat, wuv_ref[...])
        head_row = lax.broadcasted_iota(jnp.int32, full.shape, 0) % MLA_HEADS
        head_lane = lax.broadcasted_iota(jnp.int32, full.shape, 1) // MLA_V
        full = jnp.where(head_row == head_lane, full, 0.0)
        o_ref[0] = jnp.sum(full.reshape(ts, MLA_HEADS, MLA_W), axis=1)


def _mla_sample(page_table, q1, q2, pool_t, latn, w_uv_cat, ts, n_chunk):
    bs, n_pages = page_table.shape
    rows = MLA_HEADS * ts
    per_b = LANE // ts
    grid_spec = pltpu.PrefetchScalarGridSpec(
        num_scalar_prefetch=1, grid=(bs, n_pages // n_chunk),
        in_specs=[pl.BlockSpec((1, rows, 256), lambda b, c, pt: (b, 0, 0)),
                  pl.BlockSpec((1, rows, 256), lambda b, c, pt: (b, 0, 0)),
                  pl.BlockSpec(memory_space=pl.ANY),
                  pl.BlockSpec((1, 512, LANE), lambda b, c, pt: (0, 0, b // per_b)),
                  pl.BlockSpec(w_uv_cat.shape, lambda b, c, pt: (0, 0))],
        out_specs=pl.BlockSpec((1, ts, MLA_W), lambda b, c, pt: (b, 0, 0)),
        scratch_shapes=[pltpu.VMEM((2, n_chunk, LATENT, PAGE_SIZE), F32),
                        pltpu.VMEM((n_chunk, LATENT, PAGE_SIZE), BF16),
                        pltpu.VMEM((rows, n_chunk * PAGE_SIZE), F32),
                        pltpu.VMEM((rows, n_chunk * PAGE_SIZE), BF16),
                        pltpu.VMEM((rows, 1), F32), pltpu.VMEM((rows, 1), F32),
                        pltpu.VMEM((rows, MLA_KVLORA), F32),
                        pltpu.SemaphoreType.DMA((2,))])
    return pl.pallas_call(
        functools.partial(_mla_sample_kernel, ts=ts, n_pages=n_pages), grid_spec=grid_spec,
        out_shape=jax.ShapeDtypeStruct((bs, ts, MLA_W), F32),
        compiler_params=_cparams(("arbitrary", "arbitrary")), name="mla_sample",
    )(page_table, q1, q2, pool_t, latn, w_uv_cat)


def _prepare_weights(g_pre, w_in, nsa_w_cmp_k, nsa_w_cmp_v, mla_g_q, mla_w_uq, mla_g_kv, mla_w_uk, mla_w_uv,
                     mem_g, mem_w_kv, w_br_nsa, w_br_mla, w_br_mem, w_out, g_post):
    layers = []
    for l in range(w_in.shape[0]):
        w_row, w_t, w_uq = _front_weights(w_in[l], mla_w_uq[l])
        w_z, w_g, w_br = _back_weights(w_in[l], w_br_nsa[l], w_br_mla[l], w_br_mem[l])
        layers.append(dict(
            g_pre=g_pre[l], w_row=w_row, w_t=w_t, g_q=mla_g_q[l], w_uq=w_uq, g_kv=mla_g_kv[l], w_uk=mla_w_uk[l],
            w_bd=_compress_weights(nsa_w_cmp_k[l], nsa_w_cmp_v[l]), w_uv=mla_w_uv[l].astype(BF16),
            w_past=_past_compress_weights(nsa_w_cmp_k[l], nsa_w_cmp_v[l]),
            w_uv_cat=jnp.transpose(mla_w_uv[l], (1, 0, 2)).reshape(MLA_KVLORA, MLA_W).astype(BF16),
            mem_g=mem_g[l], mem_w_kv=mem_w_kv[l].astype(BF16), w_z=w_z, w_g=w_g, w_br=w_br,
            w_out=w_out[l].astype(BF16), g_post=g_post[l]))
    return layers


def _front_layer(x, pos, p, tm):
    names = ("q", "rc", "bg", "qmem", "qlat", "qrp", "rows_t", "win_t", "kv_t", "lat_t", "latx")
    outs = _front(x, pos, p["g_pre"], p["w_row"], p["w_t"], p["g_q"], p["w_uq"], p["g_kv"], p["w_uk"], tm)
    return dict(zip(names, outs))


def _back_layer(x, o_nsa, o_mla, o_mem, p, tm):
    flat = lambda a: a.reshape(-1, a.shape[-1])
    y = _back(flat(x), flat(o_nsa), flat(o_mla), flat(o_mem), p["g_pre"], p["w_z"], p["w_g"], p["w_br"],
              p["w_out"], p["g_post"], tm)
    return y.reshape(x.shape)


def _prompt_layer(x, mem_prompt, params, l):
    p = params[l]
    b, t, _ = x.shape
    f = _front_layer(x, jnp.arange(t), p, min(256, t))
    nb = t // NSA_BLK
    kcvc = _compress(f["rc"].reshape(b * nb, NSA_BLK * 256), p["w_bd"], NSA_BLK).reshape(b, nb, 256)
    o_nsa = _nsa_prompt(f["q"], f["bg"], kcvc, f["kv_t"], min(128, t))
    o_mla = _mla_prompt(f["qlat"], f["qrp"], f["latx"], p["w_uv"], min(64, t))
    n_mem = mem_prompt.shape[1]
    mem_kv = _mem_kv(mem_prompt.reshape(b * n_mem, D_MODEL), p["mem_g"], p["mem_w_kv"], min(256, b * n_mem))
    mem_kv = mem_kv.reshape(b, n_mem, 2 * MEM_W)
    o_mem = _mem_attend(f["qmem"], mem_kv, min(256, t))
    y = _back_layer(x, o_nsa, o_mla, o_mem, p, min(256, b * t))
    return dict(f, kcvc=kcvc, o_nsa=o_nsa, o_mla=o_mla, mem_kv=mem_kv, o_mem=o_mem, y=y)


def _sample_layer(x, pool_t, mla_t, win_t, mem_kv, page_table, params, l):
    p = params[l]
    bs, ts, _ = x.shape
    n_tok = bs * ts
    n_pages = page_table.shape[1]
    past_len = n_pages * PAGE_SIZE
    pos = past_len + jnp.arange(n_tok) % ts
    f = _front_layer(x.reshape(1, n_tok, D_MODEL), pos, p, min(256, n_tok))
    past = _past_compress(page_table, pool_t, p["w_past"], min(64, n_pages))
    new = _compress(f["rc"].reshape(bs, ts * 256), p["w_bd"], ts)
    new = new.reshape(bs, 2, NSA_KV, 1, NSA_HD)
    nb = past.shape[3] + 1
    nbp = -(-nb // LANE) * LANE
    kcvc = jnp.concatenate([past, new, jnp.zeros((bs, 2, NSA_KV, nbp - nb, NSA_HD), F32)], axis=3)
    qg = f["q"].reshape(bs, ts, NSA_KV, NSA_REP, NSA_HD).transpose(0, 2, 3, 1, 4).reshape(bs, NSA_KV, -1, NSA_HD)
    gates = f["bg"][0, :, :NSA_HEADS * 3].reshape(bs, ts, NSA_KV, NSA_REP, 3)
    gates = gates.transpose(0, 2, 3, 1, 4).reshape(bs, NSA_KV, -1, 3)
    o_cmp, sel = _cmp_sample(qg, kcvc, ts, past_len, nb)
    n_sel = min(NSA_TOPK, nb)
    sel = sel[:, :, :n_sel].reshape(bs, NSA_KV * ts * n_sel)
    o_nsa, win_new = _slc_sample(page_table, sel, qg, gates, o_cmp, pool_t, f["kv_t"], win_t, f["win_t"],
                                 ts, past_len)
    o_nsa = o_nsa.reshape(bs, NSA_KV, NSA_REP, ts, NSA_HD).transpose(0, 3, 1, 2, 4).reshape(bs, ts, NSA_W)
    q1 = f["qlat"][0].reshape(MLA_HEADS, bs, ts, MLA_KVLORA).transpose(1, 2, 0, 3).reshape(bs, -1, MLA_KVLORA)
    qrp = f["qrp"].reshape(bs, ts, 1, MLA_HEADS * MLA_ROPE)
    own = (jnp.arange(MLA_HEADS * MLA_ROPE) // MLA_ROPE)[None, :] == jnp.arange(MLA_HEADS)[:, None]
    q2 = jnp.where(own[None, None], qrp, jnp.zeros_like(qrp)).reshape(bs, -1, MLA_HEADS * MLA_ROPE)
    o_mla = _mla_sample(page_table, q1, q2, mla_t, f["latx"], p["w_uv_cat"], ts, min(32, n_pages))
    pad_t = 16
    qmem = jnp.pad(f["qmem"].reshape(bs, ts, MEM_W), ((0, 0), (0, pad_t - ts), (0, 0)))
    o_mem = _mem_attend(qmem, mem_kv, pad_t)[:, :ts]
    y = _back_layer(x, o_nsa, o_mla, o_mem, p, min(256, n_tok))
    return dict(f, kcvc=kcvc, o_cmp=o_cmp, sel=sel, o_nsa=o_nsa, win_new=win_new, o_mla=o_mla, o_mem=o_mem, y=y)


def kernel(x_prompt, x_sample, cache_nsa_kv, cache_mla, cache_win_kv, cache_mem_kv, page_table, mem_prompt,
           g_pre, w_in, nsa_w_cmp_k, nsa_w_cmp_v, mla_g_q, mla_w_uq, mla_g_kv, mla_w_uk, mla_w_uv,
           mem_g, mem_w_kv, w_br_nsa, w_br_mla, w_br_mem, w_out, g_post):
    params = _prepare_weights(g_pre, w_in, nsa_w_cmp_k, nsa_w_cmp_v, mla_g_q, mla_w_uq, mla_g_kv, mla_w_uk,
                              mla_w_uv, mem_g, mem_w_kv, w_br_nsa, w_br_mla, w_br_mem, w_out, g_post)
    depth = w_in.shape[0]
    bp, tp, _ = x_prompt.shape
    bs, ts, _ = x_sample.shape
    n_pool = cache_nsa_kv.shape[1]
    wb = cache_win_kv.shape[2]
    n_mem = mem_prompt.shape[1]
    wp = min(NSA_WINDOW, tp)
    xp, xs = x_prompt, x_sample
    outs = [[] for _ in range(7)]
    for l in range(depth):
        pool_t = jnp.transpose(cache_nsa_kv[l], (0, 2, 3, 4, 1)).reshape(n_pool, 4 * NSA_KV * NSA_HD, PAGE_SIZE)
        mla_t = jnp.transpose(cache_mla[l], (0, 2, 1))
        win_t = jnp.transpose(cache_win_kv[l], (0, 2, 3, 4, 1)).reshape(bs, 2 * NSA_KV * NSA_HD, wb)
        mem_kv_s = cache_mem_kv[l].reshape(bs, n_mem, 2 * MEM_W)
        rp = _prompt_layer(xp, mem_prompt, params, l)
        rs = _sample_layer(xs, pool_t, mla_t, win_t, mem_kv_s, page_table, params, l)
        xp, xs = rp["y"], rs["y"]
        outs[0].append(rp["rows_t"].reshape(bp, 4, NSA_KV, NSA_HD, tp).transpose(0, 4, 1, 2, 3))
        outs[1].append(rs["rows_t"][0].T.reshape(bs, ts, 4, NSA_KV, NSA_HD))
        outs[2].append(rp["lat_t"].transpose(0, 2, 1))
        outs[3].append(rs["lat_t"][0].T.reshape(bs, ts, LATENT))
        outs[4].append(rp["win_t"][:, :, tp - wp:].reshape(bp, 2, NSA_KV, NSA_HD, wp).transpose(0, 4, 1, 2, 3))
        outs[5].append(rs["win_new"].reshape(bs, 2, NSA_KV, NSA_HD, wb).transpose(0, 4, 1, 2, 3))
        outs[6].append(rp["mem_kv"].reshape(bp, n_mem, 2, MEM_HEADS, MEM_HD))
    return (xp, xs) + tuple(jnp.stack(o, axis=0) for o in outs)
```

```python
import functools

import numpy as np
import jax
import jax.numpy as jnp
from jax import lax
from jax.experimental import pallas as pl
from jax.experimental.pallas import tpu as pltpu

F32 = jnp.float32
BF16 = jnp.bfloat16

D_MODEL = 1024
PAGE_SIZE = 128
EPS = 1e-6
ROPE_THETA = 500000.0
NSA_HEADS = 8
NSA_KV = 2
NSA_REP = NSA_HEADS // NSA_KV
NSA_HD = 64
NSA_ROT = NSA_HD // 4
NSA_BLK = 64
NSA_TOPK = 16
NSA_WINDOW = 512
NSA_SCALE = NSA_HD ** -0.5
MLA_HEADS = 8
MLA_QLORA = 384
MLA_KVLORA = 256
MLA_NOPE = 64
MLA_ROPE = 32
MLA_V = 64
MLA_SCALE = (MLA_NOPE + MLA_ROPE) ** -0.5
LATENT = MLA_KVLORA + MLA_ROPE
MEM_HEADS = 4
MEM_HD = 128
MEM_SCALE = MEM_HD ** -0.5
NSA_W = NSA_HEADS * NSA_HD
MLA_W = MLA_HEADS * MLA_V
MEM_W = MEM_HEADS * MEM_HD
N_BRANCH = 3
NSA_KV_W = 3 * 2 * NSA_KV * NSA_HD
IN_SPLITS = (NSA_W, NSA_KV_W, NSA_HEADS * 3, NSA_W,
             MLA_QLORA, MLA_KVLORA, MLA_ROPE, MLA_W,
             MEM_W, MEM_W, N_BRANCH * D_MODEL)
NEG = -1e30
TINY = 1e-30
SEL_BONUS = 1e3
LANE = 128
ROPE_REP = LANE // MLA_ROPE
LATX = MLA_KVLORA + LANE
VMEM_LIMIT = 56 * 1024 * 1024


def _cparams(sem):
    return pltpu.CompilerParams(dimension_semantics=sem, vmem_limit_bytes=VMEM_LIMIT)


def _dot(a, b):
    return jnp.dot(a, b, preferred_element_type=F32)


def _dot_nt(a, b):
    return lax.dot_general(a, b, (((1,), (1,)), ((), ())), preferred_element_type=F32)


def _rms_scale(x):
    return lax.rsqrt(jnp.mean(x * x, axis=-1, keepdims=True) + EPS)


def _softmax_rows(s, mask):
    s = jnp.where(mask, s, NEG)
    p = jnp.where(mask, jnp.exp(s - jnp.max(s, axis=-1, keepdims=True)), 0.0)
    return p / jnp.maximum(jnp.sum(p, axis=-1, keepdims=True), TINY)


def _softmax_rows2(sa, mask_a, sb, mask_b):
    sa = jnp.where(mask_a, sa, NEG)
    sb = jnp.where(mask_b, sb, NEG)
    m = jnp.maximum(jnp.max(sa, axis=-1, keepdims=True), jnp.max(sb, axis=-1, keepdims=True))
    pa = jnp.where(mask_a, jnp.exp(sa - m), 0.0)
    pb = jnp.where(mask_b, jnp.exp(sb - m), 0.0)
    den = jnp.maximum(jnp.sum(pa, axis=-1, keepdims=True) + jnp.sum(pb, axis=-1, keepdims=True), TINY)
    return pa / den, pb / den


def _rope_lanes(x, cs_ref, group, half):
    w = x.shape[1]
    reps = w // LANE
    cos = jnp.concatenate([cs_ref[0]] * reps, axis=1)
    sin = jnp.concatenate([cs_ref[1]] * reps, axis=1)
    lane = lax.broadcasted_iota(jnp.int32, x.shape, 1)
    up = pltpu.roll(x, w - half, 1)
    down = pltpu.roll(x, half, 1)
    rot = jnp.where(lane % group < half, up, down)
    return x * cos + rot * sin


def _front_kernel(x_ref, gpre_ref, wr_ref, wt_ref, gq_ref, wuq_ref, gkv_ref, wuk_ref,
                  csn_ref, csm_ref, tn_ref, tm_ref,
                  q_ref, rc_ref, bg_ref, qmem_ref, qlat_ref, qrp_ref,
                  rows_t_ref, win_t_ref, kv_t_ref, lat_t_ref, latx_ref):
    x = x_ref[0]
    h = (x * _rms_scale(x) * gpre_ref[...]).astype(BF16)
    yr = _dot(h, wr_ref[...])
    q = _rope_lanes(yr[:, 0:NSA_W], csn_ref, NSA_HD, NSA_ROT // 2)
    q_ref[0] = (q * NSA_SCALE).astype(BF16)
    kc = _rope_lanes(yr[:, 512:640], csn_ref, NSA_HD, NSA_ROT // 2)
    rc_ref[0] = jnp.concatenate([kc, yr[:, 640:768]], axis=1).astype(BF16)
    cq = yr[:, 768:1152]
    qmem_ref[0] = yr[:, 1152:1664].astype(BF16)
    bg_ref[0] = jax.nn.sigmoid(yr[:, 1664:1792])
    cqn = (cq * _rms_scale(cq) * gq_ref[...]).astype(BF16)
    qm = _dot(cqn, wuq_ref[...])
    qrp_ref[0] = (_rope_lanes(qm[:, 512:768], csm_ref, MLA_ROPE, MLA_ROPE // 2) * MLA_SCALE).astype(BF16)
    for hd in range(MLA_HEADS):
        qn = qm[:, hd * MLA_NOPE:(hd + 1) * MLA_NOPE].astype(BF16)
        qlat_ref[0, hd] = (_dot(qn, wuk_ref[hd]) * MLA_SCALE).astype(BF16)
    yt = _dot_nt(wt_ref[...], h)
    cn, sn = tn_ref[0], tn_ref[1]
    half = NSA_ROT // 2
    pieces = []
    for base in range(0, NSA_KV_W, NSA_HD):
        is_key = (base // (NSA_KV * NSA_HD)) % 2 == 0
        if is_key:
            x1 = yt[base:base + half]
            x2 = yt[base + half:base + 2 * half]
            pieces += [x1 * cn - x2 * sn, x2 * cn + x1 * sn, yt[base + 2 * half:base + NSA_HD]]
        else:
            pieces.append(yt[base:base + NSA_HD])
    kvt = jnp.concatenate(pieces, axis=0)
    rows_t_ref[0] = kvt[0:512]
    win_t_ref[0] = kvt[512:768]
    kv_t_ref[0] = kvt[256:768].astype(BF16)
    ckv = yt[768:1024]
    ckvn = ckv * lax.rsqrt(jnp.mean(ckv * ckv, axis=0, keepdims=True) + EPS) * gkv_ref[...]
    cm, sm = tm_ref[0], tm_ref[1]
    k1 = yt[1024:1040]
    k2 = yt[1040:1056]
    kr = jnp.concatenate([k1 * cm - k2 * sm, k2 * cm + k1 * sm], axis=0)
    lat_t_ref[0] = jnp.concatenate([ckvn, kr], axis=0)
    krb = kr.astype(BF16)
    latx_ref[0] = jnp.concatenate([ckvn.astype(BF16)] + [krb] * ROPE_REP, axis=0)


def _rope_tables(pos):
    pos = pos.astype(F32)

    def tables(n_rot, group):
        half = n_rot // 2
        inv = ROPE_THETA ** (-jnp.arange(half, dtype=F32) / half)
        ang = pos[:, None] * inv[None, :]
        cos, sin = jnp.cos(ang), jnp.sin(ang)
        t = pos.shape[0]
        pad1 = jnp.ones((t, group - n_rot), F32)
        pad0 = jnp.zeros((t, group - n_rot), F32)
        cos_g = jnp.concatenate([cos, cos, pad1], axis=1)
        sin_g = jnp.concatenate([-sin, sin, pad0], axis=1)
        reps = LANE // group
        row = jnp.stack([jnp.tile(cos_g, (1, reps)), jnp.tile(sin_g, (1, reps))])
        col = jnp.stack([cos.T, sin.T])
        return row, col

    csn, tn = tables(NSA_ROT, NSA_HD)
    csm, tm = tables(MLA_ROPE, MLA_ROPE)
    return csn, csm, tn, tm


def _front_weights(w_in, mla_w_uq):
    o = np.cumsum((0,) + IN_SPLITS)
    w_q, w_kv, w_bg = w_in[:, o[0]:o[1]], w_in[:, o[1]:o[2]], w_in[:, o[2]:o[3]]
    w_cq, w_ckv, w_kr = w_in[:, o[4]:o[5]], w_in[:, o[5]:o[6]], w_in[:, o[6]:o[7]]
    w_qmem = w_in[:, o[8]:o[9]]
    pad = jnp.zeros((D_MODEL, LANE - NSA_HEADS * 3), F32)
    w_row = jnp.concatenate([w_q, w_kv[:, 0:256], w_cq, w_qmem, w_bg, pad], axis=1).astype(BF16)
    w_t = jnp.concatenate([w_kv, w_ckv, w_kr], axis=1).T.astype(BF16)
    wu = mla_w_uq.reshape(MLA_QLORA, MLA_HEADS, MLA_NOPE + MLA_ROPE)
    w_uq = jnp.concatenate([wu[:, :, :MLA_NOPE].reshape(MLA_QLORA, -1),
                            wu[:, :, MLA_NOPE:].reshape(MLA_QLORA, -1)], axis=1).astype(BF16)
    return w_row, w_t, w_uq


def _front(x, pos, g_pre, w_row, w_t, g_q, w_uq, g_kv, w_uk, tm):
    b, t, _ = x.shape
    csn, csm, tn, tmm = _rope_tables(pos)
    nt = t // tm
    tok = lambda w: pl.BlockSpec((1, tm, w), lambda i, j: (i, j, 0))
    tlay = lambda r: pl.BlockSpec((1, r, tm), lambda i, j: (i, 0, j))
    full = lambda a: pl.BlockSpec(a.shape, lambda i, j: (0,) * a.ndim)
    g_pre2, g_q2 = g_pre.reshape(1, -1), g_q.reshape(1, -1)
    g_kv2 = g_kv.reshape(-1, 1)
    w_uk = w_uk.astype(BF16)
    in_specs = [tok(D_MODEL), full(g_pre2), full(w_row), full(w_t), full(g_q2), full(w_uq), full(g_kv2),
                full(w_uk),
                pl.BlockSpec((2, tm, LANE), lambda i, j: (0, j, 0)),
                pl.BlockSpec((2, tm, LANE), lambda i, j: (0, j, 0)),
                pl.BlockSpec((2, NSA_ROT // 2, tm), lambda i, j: (0, 0, j)),
                pl.BlockSpec((2, MLA_ROPE // 2, tm), lambda i, j: (0, 0, j))]
    out_shape = [
        jax.ShapeDtypeStruct((b, t, NSA_W), BF16),
        jax.ShapeDtypeStruct((b, t, 256), BF16),
        jax.ShapeDtypeStruct((b, t, LANE), F32),
        jax.ShapeDtypeStruct((b, t, MEM_W), BF16),
        jax.ShapeDtypeStruct((b, MLA_HEADS, t, MLA_KVLORA), BF16),
        jax.ShapeDtypeStruct((b, t, MLA_HEADS * MLA_ROPE), BF16),
        jax.ShapeDtypeStruct((b, 512, t), F32),
        jax.ShapeDtypeStruct((b, 256, t), F32),
        jax.ShapeDtypeStruct((b, 512, t), BF16),
        jax.ShapeDtypeStruct((b, LATENT, t), F32),
        jax.ShapeDtypeStruct((b, LATX, t), BF16),
    ]
    out_specs = [tok(NSA_W), tok(256), tok(LANE), tok(MEM_W),
                 pl.BlockSpec((1, MLA_HEADS, tm, MLA_KVLORA), lambda i, j: (i, 0, j, 0)),
                 tok(MLA_HEADS * MLA_ROPE),
                 tlay(512), tlay(256), tlay(512), tlay(LATENT), tlay(LATX)]
    return pl.pallas_call(
        _front_kernel, grid=(b, nt), in_specs=in_specs, out_specs=out_specs, out_shape=out_shape,
        compiler_params=_cparams(("parallel", "parallel")), name="front",
    )(x, g_pre2, w_row, w_t, g_q2, w_uq, g_kv2, w_uk, csn, csm, tn, tmm)


def _compress_weights(wk, wv):
    z = jnp.zeros_like(wk)
    rows = [jnp.concatenate(r, axis=2) for r in ([wk, z, z, z], [z, wk, z, z], [z, z, wv, z], [z, z, z, wv])]
    return jnp.concatenate(rows, axis=1).astype(BF16)


def _compress_kernel(x_ref, w_ref, o_ref):
    @pl.when(pl.program_id(0) == 0)
    def _():
        o_ref[...] = jnp.zeros_like(o_ref)

    o_ref[...] += _dot(x_ref[...], w_ref[0])


def _compress(x, w_bd, n_l):
    r = x.shape[0]
    return pl.pallas_call(
        _compress_kernel, grid=(n_l,),
        in_specs=[pl.BlockSpec((r, 256), lambda l: (0, l)),
                  pl.BlockSpec((1, 256, 256), lambda l: (l, 0, 0))],
        out_specs=pl.BlockSpec((r, 256), lambda l: (0, 0)),
        out_shape=jax.ShapeDtypeStruct((r, 256), F32),
        compiler_params=_cparams(("arbitrary",)), name="nsa_compress",
    )(x, w_bd)


def _select_blocks(score, nb, n_sel):
    r = score.shape[0]
    st = score.T[0:nb]
    blk = lax.broadcasted_iota(jnp.int32, (nb, 1), 0)
    rank = jnp.zeros((nb, r), F32)
    for m in range(nb):
        sm = st[m:m + 1]
        rank = rank + jnp.where((sm > st) | ((sm == st) & (m < blk)), 1.0, 0.0)
    sel_t = jnp.where(rank < n_sel, 1.0, 0.0)
    return jnp.concatenate([sel_t, jnp.zeros((LANE - nb, r), F32)], axis=0).T


ATT_KC = 512


def _lane_tiles(x):
    return [x[:, j * LANE:(j + 1) * LANE] for j in range(x.shape[1] // LANE)]


def _tile_max(x):
    return functools.reduce(jnp.maximum, _lane_tiles(x))


def _nsa_prompt_kernel(q_ref, bg_ref, kcvc_ref, kv_ref, o_ref, bias_ref, s_buf, mx_ref, l_ref, acc_ref,
                       *, tq, t_len):
    nb = t_len // NSA_BLK
    kc = min(ATT_KC, t_len)
    q0 = pl.program_id(1) * tq
    n_kc = (q0 + tq + kc - 1) // kc
    qpos = q0 + lax.broadcasted_iota(jnp.int32, (tq, 1), 0)
    blk = lax.broadcasted_iota(jnp.int32, (1, LANE), 1)
    vis = ((blk + 1) * NSA_BLK <= qpos + 1) & (blk < nb)
    cur = qpos // NSA_BLK
    valid = (blk <= cur) & (blk < nb)
    forced = (blk == 0) | (blk == cur) | (blk == cur - 1)
    key = lax.broadcasted_iota(jnp.int32, (1, t_len), 1)
    expand = (lax.broadcasted_iota(jnp.int32, (LANE, t_len), 1) // NSA_BLK
              == lax.broadcasted_iota(jnp.int32, (LANE, t_len), 0)).astype(BF16)
    wlen = min(NSA_WINDOW + tq, t_len)
    wstart = pl.multiple_of(jnp.maximum(q0 - NSA_WINDOW, 0), LANE)
    wdist = qpos - (wstart + lax.broadcasted_iota(jnp.int32, (1, wlen), 1))
    wbias = jnp.where((wdist >= 0) & (wdist < NSA_WINDOW), 0.0, NEG)
    q = q_ref[0]
    gates = bg_ref[0]
    kcvc = kcvc_ref[0].astype(BF16)
    qs = [q[:, h * NSA_HD:(h + 1) * NSA_HD] for h in range(NSA_HEADS)]
    o_cmp = []
    for g in range(NSA_KV):
        kcg = kcvc[:, g * NSA_HD:(g + 1) * NSA_HD]
        vcg = kcvc[:, 128 + g * NSA_HD:128 + (g + 1) * NSA_HD]
        psum = jnp.zeros((tq, LANE), F32)
        for r in range(NSA_REP):
            p = _softmax_rows(_dot_nt(qs[g * NSA_REP + r], kcg), vis)
            o_cmp.append(_dot(p.astype(BF16), vcg))
            psum = psum + p
        score = jnp.where(valid, psum + jnp.where(forced, SEL_BONUS, 0.0), -jnp.inf)
        sel = _select_blocks(score, nb, min(NSA_TOPK, nb))
        bias_ref[g] = jnp.where((_dot(sel.astype(BF16), expand) > 0.5) & (key <= qpos), 0.0, NEG)
    mx_ref[...] = jnp.full(mx_ref.shape, NEG, F32)
    l_ref[...] = jnp.zeros(l_ref.shape, F32)
    acc_ref[...] = jnp.zeros(acc_ref.shape, F32)

    def pass1(c, carry):
        k0 = pl.multiple_of(c * kc, kc)
        for h in range(NSA_HEADS):
            g = h // NSA_REP
            s = _dot(qs[h], kv_ref[0, g * NSA_HD:(g + 1) * NSA_HD, pl.ds(k0, kc)]) + bias_ref[g, :, pl.ds(k0, kc)]
            s_buf[h, :, pl.ds(k0, kc)] = s
            mx_ref[h] = jnp.maximum(mx_ref[h], _tile_max(s))
        return carry

    lax.fori_loop(0, n_kc, pass1, 0)
    for h in range(NSA_HEADS):
        mx_ref[h] = jnp.broadcast_to(jnp.max(mx_ref[h], axis=1, keepdims=True), (tq, LANE))

    def pass2(c, carry):
        k0 = pl.multiple_of(c * kc, kc)
        for h in range(NSA_HEADS):
            g = h // NSA_REP
            m = mx_ref[h]
            ps = [jnp.exp(t - m) for t in _lane_tiles(s_buf[h, :, pl.ds(k0, kc)])]
            l_ref[h] += functools.reduce(jnp.add, ps)
            p = jnp.concatenate(ps, axis=1).astype(BF16)
            acc_ref[h] += _dot_nt(p, kv_ref[0, 128 + g * NSA_HD:128 + (g + 1) * NSA_HD, pl.ds(k0, kc)])
        return carry

    lax.fori_loop(0, n_kc, pass2, 0)
    outs = []
    for h in range(NSA_HEADS):
        g = h // NSA_REP
        o_slc = acc_ref[h] / jnp.sum(l_ref[h], axis=1, keepdims=True)
        kw = kv_ref[0, 256 + g * NSA_HD:256 + (g + 1) * NSA_HD, pl.ds(wstart, wlen)]
        vw = kv_ref[0, 384 + g * NSA_HD:384 + (g + 1) * NSA_HD, pl.ds(wstart, wlen)]
        s = _dot(qs[h], kw) + wbias
        p = jnp.exp(s - jnp.max(s, axis=1, keepdims=True))
        o_win = _dot_nt(p.astype(BF16), vw) / jnp.sum(p, axis=1, keepdims=True)
        outs.append(gates[:, 3 * h:3 * h + 1] * o_cmp[h] + gates[:, 3 * h + 1:3 * h + 2] * o_slc
                    + gates[:, 3 * h + 2:3 * h + 3] * o_win)
    o_ref[0] = jnp.concatenate(outs, axis=1)


def _nsa_prompt(q, bg, kcvc, kv_t, tq):
    b, t, _ = q.shape
    tok = lambda w: pl.BlockSpec((1, tq, w), lambda i, j: (i, j, 0))
    return pl.pallas_call(
        functools.partial(_nsa_prompt_kernel, tq=tq, t_len=t), grid=(b, t // tq),
        in_specs=[tok(NSA_W), tok(LANE),
                  pl.BlockSpec((1, LANE, 256), lambda i, j: (i, 0, 0)),
                  pl.BlockSpec((1, 512, t), lambda i, j: (i, 0, 0))],
        out_specs=tok(NSA_W),
        out_shape=jax.ShapeDtypeStruct((b, t, NSA_W), F32),
        scratch_shapes=[pltpu.VMEM((NSA_KV, tq, t), F32), pltpu.VMEM((NSA_HEADS, tq, t), F32),
                        pltpu.VMEM((NSA_HEADS, tq, LANE), F32), pltpu.VMEM((NSA_HEADS, tq, LANE), F32),
                        pltpu.VMEM((NSA_HEADS, tq, NSA_HD), F32)],
        compiler_params=_cparams(("parallel", "parallel")), name="nsa_prompt",
    )(q, bg, kcvc, kv_t)


def _stack_rope_queries(qrp):
    r = qrp.shape[0]
    slot = lax.broadcasted_iota(jnp.int32, (r, LANE), 1) // MLA_ROPE
    parts = []
    for h in range(MLA_HEADS):
        tile = qrp[:, (h // ROPE_REP) * LANE:(h // ROPE_REP + 1) * LANE]
        parts.append(jnp.where(slot == h % ROPE_REP, tile, jnp.zeros_like(tile)))
    return jnp.concatenate(parts, axis=0)


def _mla_prompt_kernel(qlat_ref, qrp_ref, latx_ref, wuv_ref, o_ref, s_buf, mx_ref, l_ref, acc_ref, *, tq, t_len):
    kc = min(ATT_KC, t_len)
    rows = MLA_HEADS * tq
    q0 = pl.program_id(1) * tq
    n_full = q0 // kc
    q1 = qlat_ref[0].reshape(rows, MLA_KVLORA)
    q2 = _stack_rope_queries(qrp_ref[0])

    def scores(c):
        k0 = pl.multiple_of(c * kc, kc)
        return (_dot(q1, latx_ref[0, 0:MLA_KVLORA, pl.ds(k0, kc)])
                + _dot(q2, latx_ref[0, MLA_KVLORA:, pl.ds(k0, kc)]))

    mx_ref[...] = jnp.full(mx_ref.shape, NEG, F32)
    l_ref[...] = jnp.zeros(l_ref.shape, F32)
    acc_ref[...] = jnp.zeros(acc_ref.shape, F32)

    def pass1(c, carry):
        s = scores(c)
        s_buf[:, pl.ds(pl.multiple_of(c * kc, kc), kc)] = s
        mx_ref[...] = jnp.maximum(mx_ref[...], _tile_max(s))
        return carry

    lax.fori_loop(0, n_full, pass1, 0)
    qpos = q0 + lax.broadcasted_iota(jnp.int32, (rows, 1), 0) % tq
    key = n_full * kc + lax.broadcasted_iota(jnp.int32, (1, kc), 1)
    s = scores(n_full) + jnp.where(key <= qpos, 0.0, NEG)
    s_buf[:, pl.ds(pl.multiple_of(n_full * kc, kc), kc)] = s
    m = jnp.max(jnp.maximum(mx_ref[...], _tile_max(s)), axis=1, keepdims=True)
    mx_ref[...] = jnp.broadcast_to(m, (rows, LANE))

    def pass2(c, carry):
        k0 = pl.multiple_of(c * kc, kc)
        mm = mx_ref[...]
        ps = [jnp.exp(t - mm) for t in _lane_tiles(s_buf[:, pl.ds(k0, kc)])]
        l_ref[...] += functools.reduce(jnp.add, ps)
        p = jnp.concatenate(ps, axis=1).astype(BF16)
        acc_ref[...] += _dot_nt(p, latx_ref[0, 0:MLA_KVLORA, pl.ds(k0, kc)])
        return carry

    lax.fori_loop(0, n_full + 1, pass2, 0)
    o_lat = (acc_ref[...] / jnp.sum(l_ref[...], axis=1, keepdims=True)).astype(BF16)
    o_ref[0] = jnp.concatenate([_dot(o_lat[h * tq:(h + 1) * tq], wuv_ref[h]) for h in range(MLA_HEADS)], axis=1)


def _mla_prompt(qlat, qrp, latx, w_uv, tq):
    b, _, t, _ = qlat.shape
    return pl.pallas_call(
        functools.partial(_mla_prompt_kernel, tq=tq, t_len=t), grid=(b, t // tq),
        in_specs=[pl.BlockSpec((1, MLA_HEADS, tq, MLA_KVLORA), lambda i, j: (i, 0, j, 0)),
                  pl.BlockSpec((1, tq, 256), lambda i, j: (i, j, 0)),
                  pl.BlockSpec((1, LATX, t), lambda i, j: (i, 0, 0)),
                  pl.BlockSpec(w_uv.shape, lambda i, j: (0, 0, 0))],
        out_specs=pl.BlockSpec((1, tq, MLA_W), lambda i, j: (i, j, 0)),
        out_shape=jax.ShapeDtypeStruct((b, t, MLA_W), F32),
        scratch_shapes=[pltpu.VMEM((MLA_HEADS * tq, t), F32), pltpu.VMEM((MLA_HEADS * tq, LANE), F32),
                        pltpu.VMEM((MLA_HEADS * tq, LANE), F32), pltpu.VMEM((MLA_HEADS * tq, MLA_KVLORA), F32)],
        compiler_params=_cparams(("parallel", "parallel")), name="mla_prompt",
    )(qlat, qrp, latx, w_uv)


def _mem_kv_kernel(m_ref, g_ref, w_ref, o_ref):
    m = m_ref[...]
    o_ref[...] = _dot((m * _rms_scale(m) * g_ref[...]).astype(BF16), w_ref[...])


def _mem_kv(mem, g, w, tm):
    n = mem.shape[0]
    g2 = g.reshape(1, -1)
    return pl.pallas_call(
        _mem_kv_kernel, grid=(n // tm,),
        in_specs=[pl.BlockSpec((tm, D_MODEL), lambda i: (i, 0)),
                  pl.BlockSpec(g2.shape, lambda i: (0, 0)),
                  pl.BlockSpec(w.shape, lambda i: (0, 0))],
        out_specs=pl.BlockSpec((tm, 2 * MEM_W), lambda i: (i, 0)),
        out_shape=jax.ShapeDtypeStruct((n, 2 * MEM_W), F32),
        compiler_params=_cparams(("parallel",)), name="mem_kv",
    )(mem, g2, w)


def _mem_attend_kernel(q_ref, kv_ref, o_ref):
    q = q_ref[0]
    outs = []
    for h in range(MEM_HEADS):
        k = kv_ref[0, :, h * MEM_HD:(h + 1) * MEM_HD].astype(BF16)
        v = kv_ref[0, :, MEM_W + h * MEM_HD:MEM_W + (h + 1) * MEM_HD].astype(BF16)
        s = _dot_nt(q[:, h * MEM_HD:(h + 1) * MEM_HD], k) * MEM_SCALE
        e = jnp.exp(s - jnp.max(s, axis=-1, keepdims=True))
        p = e / jnp.sum(e, axis=-1, keepdims=True)
        outs.append(_dot(p.astype(BF16), v))
    o_ref[0] = jnp.concatenate(outs, axis=1)


def _mem_attend(q, kv, tq):
    b, t, _ = q.shape
    return pl.pallas_call(
        _mem_attend_kernel, grid=(b, t // tq),
        in_specs=[pl.BlockSpec((1, tq, MEM_W), lambda i, j: (i, j, 0)),
                  pl.BlockSpec((1,) + kv.shape[1:], lambda i, j: (i, 0, 0))],
        out_specs=pl.BlockSpec((1, tq, MEM_W), lambda i, j: (i, j, 0)),
        out_shape=jax.ShapeDtypeStruct((b, t, MEM_W), F32),
        compiler_params=_cparams(("parallel", "parallel")), name="mem_attend",
    )(q, kv)


def _back_kernel(x_ref, on_ref, om_ref, oe_ref, gpre_ref, wz_ref, wg_ref, wbr_ref, wout_ref, gpost_ref, y_ref):
    x = x_ref[...]
    h = (x * _rms_scale(x) * gpre_ref[...]).astype(BF16)
    y = jnp.zeros(x.shape, F32)
    for j, o_ref in enumerate((on_ref, om_ref, oe_ref)):
        z = _dot(h, wz_ref[j])
        u = (o_ref[...] * (z * jax.nn.sigmoid(z))).astype(BF16)
        gate = jax.nn.sigmoid(_dot(h, wg_ref[j]))
        y = y + gate * _dot(u, wbr_ref[j])
    v = _dot(y.astype(BF16), wout_ref[...])
    y_ref[...] = x + v * _rms_scale(v) * gpost_ref[...]


def _back_weights(w_in, w_br_nsa, w_br_mla, w_br_mem):
    o = np.cumsum((0,) + IN_SPLITS)
    w_z = jnp.stack([w_in[:, o[3]:o[4]], w_in[:, o[7]:o[8]], w_in[:, o[9]:o[10]]]).astype(BF16)
    w_g = jnp.stack([w_in[:, o[10] + j * D_MODEL:o[10] + (j + 1) * D_MODEL] for j in range(N_BRANCH)]).astype(BF16)
    w_br = jnp.stack([w_br_nsa, w_br_mla, w_br_mem]).astype(BF16)
    return w_z, w_g, w_br


def _back(x, o_nsa, o_mla, o_mem, g_pre, w_z, w_g, w_br, w_out, g_post, tm):
    n = x.shape[0]
    g_pre2, g_post2 = g_pre.reshape(1, -1), g_post.reshape(1, -1)
    full = lambda a: pl.BlockSpec(a.shape, lambda i: (0,) * a.ndim)
    tok = lambda w: pl.BlockSpec((tm, w), lambda i: (i, 0))
    return pl.pallas_call(
        _back_kernel, grid=(n // tm,),
        in_specs=[tok(D_MODEL), tok(512), tok(512), tok(512), full(g_pre2), full(w_z), full(w_g), full(w_br),
                  full(w_out), full(g_post2)],
        out_specs=tok(D_MODEL),
        out_shape=jax.ShapeDtypeStruct((n, D_MODEL), F32),
        compiler_params=_cparams(("parallel",)), name="back",
    )(x, o_nsa, o_mla, o_mem, g_pre2, w_z, w_g, w_br, w_out, g_post2)


def _pipelined_gather(step, n_steps, n_copies, make_copies):
    slot = step % 2

    def run(s, sl, start):
        def body(i, carry):
            for cp in make_copies(s, i, sl):
                if start:
                    cp.start()
                else:
                    cp.wait()
            return carry
        lax.fori_loop(0, n_copies, body, 0)

    @pl.when(step == 0)
    def _():
        run(step, slot, True)

    @pl.when(step + 1 < n_steps)
    def _():
        run(step + 1, 1 - slot, True)

    run(step, slot, False)
    return slot


def _past_compress_weights(wk, wv):
    def one(w):
        return jnp.einsum('jk,lde->djlke', jnp.eye(2, dtype=F32), w).reshape(NSA_HD * PAGE_SIZE, 128)
    return jnp.stack([one(wk), one(wv)]).astype(BF16)


def _past_compress_kernel(pt_ref, pool_ref, w_ref, o_ref, buf, a_buf, sem, *, n_pages):
    chunk = buf.shape[2]
    steps_per_b = n_pages // chunk
    step = pl.program_id(0) * steps_per_b + pl.program_id(1)

    def page_copy(s, i, slot):
        page = pt_ref[s // steps_per_b, (s % steps_per_b) * chunk + i]
        return (pltpu.make_async_copy(pool_ref.at[page, pl.ds(0, 256), :], buf.at[slot, :, i, :], sem.at[slot]),)

    slot = _pipelined_gather(step, pl.num_programs(0) * steps_per_b, chunk, page_copy)

    for kv in range(2):
        for g in range(NSA_KV):
            for d in range(NSA_HD):
                a_buf[g * chunk:(g + 1) * chunk, d * PAGE_SIZE:(d + 1) * PAGE_SIZE] = (
                    buf[slot, kv * 128 + g * NSA_HD + d].astype(BF16))
        acc = _dot(a_buf[...], w_ref[kv])
        for g in range(NSA_KV):
            o_ref[0, kv, g] = acc[g * chunk:(g + 1) * chunk]


def _past_compress(page_table, pool_t, w2, chunk):
    bs, n_pages = page_table.shape
    grid_spec = pltpu.PrefetchScalarGridSpec(
        num_scalar_prefetch=1, grid=(bs, n_pages // chunk),
        in_specs=[pl.BlockSpec(memory_space=pl.ANY),
                  pl.BlockSpec(w2.shape, lambda b, c, pt: (0, 0, 0))],
        out_specs=pl.BlockSpec((1, 2, NSA_KV, chunk, 128), lambda b, c, pt: (b, 0, 0, c, 0)),
        scratch_shapes=[pltpu.VMEM((2, 256, chunk, PAGE_SIZE), F32),
                        pltpu.VMEM((NSA_KV * chunk, NSA_HD * PAGE_SIZE), BF16),
                        pltpu.SemaphoreType.DMA((2,))])
    out = pl.pallas_call(
        functools.partial(_past_compress_kernel, n_pages=n_pages), grid_spec=grid_spec,
        out_shape=jax.ShapeDtypeStruct((bs, 2, NSA_KV, n_pages, 128), F32),
        compiler_params=_cparams(("arbitrary", "arbitrary")), name="past_compress",
    )(page_table, pool_t, w2)
    return out.reshape(bs, 2, NSA_KV, n_pages * (PAGE_SIZE // NSA_BLK), NSA_HD)


def _cmp_sample_kernel(q_ref, kcvc_ref, oc_ref, sel_ref, sc_ref, *, ts, past_len, nb):
    nbp = kcvc_ref.shape[3]
    n_b = q_ref.shape[0]
    rows = NSA_REP * ts
    t_row = lax.broadcasted_iota(jnp.int32, (rows, 1), 0) % ts
    blk = lax.broadcasted_iota(jnp.int32, (1, nbp), 1)
    vis = ((blk + 1) * NSA_BLK <= past_len + t_row + 1) & (blk < nb)
    for bb in range(n_b):
        for g in range(NSA_KV):
            kc = kcvc_ref[bb, 0, g].astype(BF16)
            vc = kcvc_ref[bb, 1, g].astype(BF16)
            p = _softmax_rows(_dot_nt(q_ref[bb, g], kc), vis)
            oc_ref[bb, g] = _dot(p.astype(BF16), vc)
            psum = p[0:ts]
            for r in range(1, NSA_REP):
                psum = psum + p[r * ts:(r + 1) * ts]
            sc_ref[(bb * NSA_KV + g) * ts:(bb * NSA_KV + g + 1) * ts, :] = psum
    n_rows = n_b * NSA_KV * ts
    t_sel = lax.broadcasted_iota(jnp.int32, (n_rows, 1), 0) % ts
    cur = (past_len + t_sel) // NSA_BLK
    valid = (blk <= cur) & (blk < nb)
    forced = (blk == 0) | (blk == cur) | (blk == cur - 1)
    score = jnp.where(valid, sc_ref[...] + jnp.where(forced, SEL_BONUS, 0.0), -jnp.inf)
    blk_f = blk.astype(F32)
    taken = jnp.broadcast_to(blk >= nb, score.shape)
    col = lax.broadcasted_iota(jnp.int32, (n_rows, LANE), 1)
    sel = jnp.zeros((n_rows, LANE), F32)
    for k in range(min(NSA_TOPK, nb)):
        best = jnp.max(jnp.where(taken, -jnp.inf, score), axis=1, keepdims=True)
        idx = jnp.min(jnp.where((~taken) & (score == best), blk_f, float(nbp)), axis=1, keepdims=True)
        sel = jnp.where(col == k, idx, sel)
        taken = taken | (blk_f == idx)
    sel_ref[...] = sel.astype(jnp.int32).reshape(sel_ref.shape)


def _cmp_sample(qg, kcvc, ts, past_len, nb, n_b):
    bs = qg.shape[0]
    rows = NSA_REP * ts
    nbp = kcvc.shape[3]
    return pl.pallas_call(
        functools.partial(_cmp_sample_kernel, ts=ts, past_len=past_len, nb=nb), grid=(bs // n_b,),
        in_specs=[pl.BlockSpec((n_b, NSA_KV, rows, NSA_HD), lambda b: (b, 0, 0, 0)),
                  pl.BlockSpec((n_b, 2, NSA_KV, nbp, NSA_HD), lambda b: (b, 0, 0, 0, 0))],
        out_specs=[pl.BlockSpec((n_b, NSA_KV, rows, NSA_HD), lambda b: (b, 0, 0, 0)),
                   pl.BlockSpec((n_b, NSA_KV * ts, LANE), lambda b: (b, 0, 0))],
        out_shape=[jax.ShapeDtypeStruct((bs, NSA_KV, rows, NSA_HD), F32),
                   jax.ShapeDtypeStruct((bs, NSA_KV * ts, LANE), jnp.int32)],
        scratch_shapes=[pltpu.VMEM((n_b * NSA_KV * ts, nbp), F32)],
        compiler_params=_cparams(("parallel",)), name="cmp_sample",
    )(qg, kcvc)


def _slc_sample_kernel(pt_ref, sel_ref, q_ref, gate_ref, oc_ref, pool_ref, kvn_ref, win_ref, winn_ref,
                       o_ref, wout_ref, kbuf, vbuf, sem, *, ts, past_len, n_sel):
    b = pl.program_id(0)
    nbs = pl.num_programs(0)
    bpp = PAGE_SIZE // NSA_BLK
    nbp = pt_ref.shape[1] * bpp
    n_tg = NSA_KV * ts
    rows = NSA_REP * ts
    per_b = LANE // ts

    def copies(bb, gt, slot):
        krow = pl.multiple_of(256 + (gt // ts) * NSA_HD, NSA_HD)
        vrow = pl.multiple_of(384 + (gt // ts) * NSA_HD, NSA_HD)
        out = []
        for k in range(n_sel):
            s = jnp.minimum(sel_ref[bb, gt * n_sel + k], nbp - 1)
            page = pt_ref[bb, s // bpp]
            dst = pl.ds(k * PAGE_SIZE, PAGE_SIZE)
            out.append(pltpu.make_async_copy(pool_ref.at[page, pl.ds(krow, NSA_HD), :],
                                             kbuf.at[slot, gt, :, dst], sem.at[slot]))
            out.append(pltpu.make_async_copy(pool_ref.at[page, pl.ds(vrow, NSA_HD), :],
                                             vbuf.at[slot, gt, :, dst], sem.at[slot]))
        return out

    slot = _pipelined_gather(b, nbs, n_tg, copies)

    t_row = lax.broadcasted_iota(jnp.int32, (rows, 1), 0) % ts
    lane_p = lax.broadcasted_iota(jnp.int32, (1, n_sel * PAGE_SIZE), 1)
    lane_n = lax.broadcasted_iota(jnp.int32, (1, LANE), 1)
    mine = lane_n // ts == b % per_b
    t_key = lane_n % ts
    wb = win_ref.shape[2]
    lane_w = lax.broadcasted_iota(jnp.int32, (1, wb), 1)
    wdist = t_row + wb - lane_w
    wmask = (wdist >= 0) & (wdist < NSA_WINDOW)
    wdist_n = t_row - t_key
    wmask_n = mine & (wdist_n >= 0) & (wdist_n < NSA_WINDOW)
    for g in range(NSA_KV):
        q = q_ref[0, g]
        gates = gate_ref[0, g]
        kn = kvn_ref[0, g * NSA_HD:(g + 1) * NSA_HD, :]
        vn = kvn_ref[0, 128 + g * NSA_HD:128 + (g + 1) * NSA_HD, :]
        s_new = _dot(q, kn)
        o_slc = jnp.zeros((rows, NSA_HD), F32)
        for t in range(ts):
            selv = jnp.zeros((1, n_sel * PAGE_SIZE), jnp.int32)
            chosen = jnp.zeros((1, LANE), jnp.bool_)
            for k in range(n_sel):
                s_k = sel_ref[b, (g * ts + t) * n_sel + k]
                selv = jnp.where(lane_p // PAGE_SIZE == k, s_k, selv)
                chosen = chosen | ((past_len + t_key) // NSA_BLK == s_k)
            pmask = (selv < nbp) & ((lane_p % PAGE_SIZE) // NSA_BLK == selv % bpp) & (t_row == t)
            nmask = mine & chosen & (t_key <= t_row) & (t_row == t)
            kt = kbuf[slot, g * ts + t].astype(BF16)
            vt = vbuf[slot, g * ts + t].astype(BF16)
            pa, pb = _softmax_rows2(_dot(q, kt), pmask, s_new, nmask)
            o_slc = o_slc + _dot_nt(pa.astype(BF16), vt) + _dot_nt(pb.astype(BF16), vn)
        kw = win_ref[0, g * NSA_HD:(g + 1) * NSA_HD, :].astype(BF16)
        vw = win_ref[0, 128 + g * NSA_HD:128 + (g + 1) * NSA_HD, :].astype(BF16)
        kwn = kvn_ref[0, 256 + g * NSA_HD:256 + (g + 1) * NSA_HD, :]
        vwn = kvn_ref[0, 384 + g * NSA_HD:384 + (g + 1) * NSA_HD, :]
        pa, pb = _softmax_rows2(_dot(q, kw), wmask, _dot(q, kwn), wmask_n)
        o_win = _dot_nt(pa.astype(BF16), vw) + _dot_nt(pb.astype(BF16), vwn)
        o_ref[0, g] = gates[:, 0:1] * oc_ref[0, g] + gates[:, 1:2] * o_slc + gates[:, 2:3] * o_win
    shifted = pltpu.roll(win_ref[0], wb - ts, 1)
    wout_ref[0] = shifted
    new_cols = pltpu.roll(winn_ref[0], (LANE - ts - (b % per_b) * ts) % LANE, 1)
    wout_ref[0, :, wb - LANE:] = jnp.where(lane_n >= LANE - ts, new_cols, shifted[:, wb - LANE:])


def _slc_sample(page_table, sel, qg, gates, o_cmp, pool_t, kvn_t, win_t, winn_t, ts, past_len):
    bs = qg.shape[0]
    rows = NSA_REP * ts
    n_sel = sel.shape[1] // (NSA_KV * ts)
    wb = win_t.shape[2]
    per_b = LANE // ts
    qspec = lambda w: pl.BlockSpec((1, NSA_KV, rows, w), lambda b, pt, sl: (b, 0, 0, 0))
    grid_spec = pltpu.PrefetchScalarGridSpec(
        num_scalar_prefetch=2, grid=(bs,),
        in_specs=[qspec(NSA_HD), qspec(3), qspec(NSA_HD),
                  pl.BlockSpec(memory_space=pl.ANY),
                  pl.BlockSpec((1, 512, LANE), lambda b, pt, sl: (0, 0, b // per_b)),
                  pl.BlockSpec((1, 256, wb), lambda b, pt, sl: (b, 0, 0)),
                  pl.BlockSpec((1, 256, LANE), lambda b, pt, sl: (0, 0, b // per_b))],
        out_specs=[qspec(NSA_HD), pl.BlockSpec((1, 256, wb), lambda b, pt, sl: (b, 0, 0))],
        scratch_shapes=[pltpu.VMEM((2, NSA_KV * ts, NSA_HD, n_sel * PAGE_SIZE), F32),
                        pltpu.VMEM((2, NSA_KV * ts, NSA_HD, n_sel * PAGE_SIZE), F32),
                        pltpu.SemaphoreType.DMA((2,))])
    return pl.pallas_call(
        functools.partial(_slc_sample_kernel, ts=ts, past_len=past_len, n_sel=n_sel), grid_spec=grid_spec,
        out_shape=[jax.ShapeDtypeStruct((bs, NSA_KV, rows, NSA_HD), F32),
                   jax.ShapeDtypeStruct((bs, 256, wb), F32)],
        compiler_params=_cparams(("arbitrary",)), name="slc_sample",
    )(page_table, sel, qg, gates, o_cmp, pool_t, kvn_t, win_t, winn_t)


def _mla_sample_kernel(pt_ref, q1_ref, q2_ref, pool_ref, latn_ref, wuv_ref, o_ref,
                       buf, kb, m_ref, l_ref, acc_ref, sem, *, ts, n_pages):
    n_chunk = buf.shape[2] // PAGE_SIZE
    steps_per_b = n_pages // n_chunk
    b, c = pl.program_id(0), pl.program_id(1)
    step = b * steps_per_b + c
    rows = MLA_HEADS * ts

    def page_copy(s, i, slot):
        page = pt_ref[s // steps_per_b, (s % steps_per_b) * n_chunk + i]
        dst = buf.at[slot, :, pl.ds(pl.multiple_of(i * PAGE_SIZE, PAGE_SIZE), PAGE_SIZE)]
        return (pltpu.make_async_copy(pool_ref.at[page], dst, sem.at[slot]),)

    slot = _pipelined_gather(step, pl.num_programs(0) * steps_per_b, n_chunk, page_copy)

    @pl.when(step == 0)
    def _():
        kb[LATENT:, :] = jnp.zeros((LATX - LATENT, kb.shape[1]), BF16)

    @pl.when(c == 0)
    def _():
        m_ref[...] = jnp.full(m_ref.shape, NEG, F32)
        l_ref[...] = jnp.zeros(l_ref.shape, F32)
        acc_ref[...] = jnp.zeros(acc_ref.shape, F32)

    q1 = q1_ref[0]
    q2 = q2_ref[0]

    def update(s, values):
        m_old = m_ref[...]
        m_new = jnp.maximum(m_old, jnp.max(s, axis=-1, keepdims=True))
        alpha = jnp.exp(m_old - m_new)
        p = jnp.exp(s - m_new)
        m_ref[...] = m_new
        l_ref[...] = l_ref[...] * alpha + jnp.sum(p, axis=-1, keepdims=True)
        acc_ref[...] = acc_ref[...] * alpha + _dot_nt(p.astype(BF16), values)

    kb[0:LATENT, :] = buf[slot].astype(BF16)
    update(_dot(q1, kb[0:MLA_KVLORA, :]) + _dot(q2, kb[MLA_KVLORA:, :]), kb[0:MLA_KVLORA, :])

    @pl.when(c == steps_per_b - 1)
    def _():
        latn = latn_ref[0]
        per_b = LANE // ts
        lane_n = lax.broadcasted_iota(jnp.int32, (1, LANE), 1)
        t_row = lax.broadcasted_iota(jnp.int32, (rows, 1), 0) // MLA_HEADS
        mask = (lane_n // ts == b % per_b) & (lane_n % ts <= t_row)
        s = _dot(q1, latn[0:MLA_KVLORA]) + _dot(q2, latn[MLA_KVLORA:])
        update(jnp.where(mask, s, NEG), latn[0:MLA_KVLORA])
        o_lat = (acc_ref[...] / l_ref[...]).astype(BF16)
        full = _dot(o_lat, wuv_ref[...])
        head_row = lax.broadcasted_iota(jnp.int32, full.shape, 0) % MLA_HEADS
        head_lane = lax.broadcasted_iota(jnp.int32, full.shape, 1) // MLA_V
        full = jnp.where(head_row == head_lane, full, 0.0)
        o_ref[0] = jnp.sum(full.reshape(ts, MLA_HEADS, MLA_W), axis=1)


def _mla_sample(page_table, q1, q2, pool_t, latn, w_uv_cat, ts, n_chunk):
    bs, n_pages = page_table.shape
    rows = MLA_HEADS * ts
    per_b = LANE // ts
    grid_spec = pltpu.PrefetchScalarGridSpec(
        num_scalar_prefetch=1, grid=(bs, n_pages // n_chunk),
        in_specs=[pl.BlockSpec((1, rows, MLA_KVLORA), lambda b, c, pt: (b, 0, 0)),
                  pl.BlockSpec((1, rows, LANE), lambda b, c, pt: (b, 0, 0)),
                  pl.BlockSpec(memory_space=pl.ANY),
                  pl.BlockSpec((1, LATX, LANE), lambda b, c, pt: (0, 0, b // per_b)),
                  pl.BlockSpec(w_uv_cat.shape, lambda b, c, pt: (0, 0))],
        out_specs=pl.BlockSpec((1, ts, MLA_W), lambda b, c, pt: (b, 0, 0)),
        scratch_shapes=[pltpu.VMEM((2, LATENT, n_chunk * PAGE_SIZE), F32),
                        pltpu.VMEM((LATX, n_chunk * PAGE_SIZE), BF16),
                        pltpu.VMEM((rows, 1), F32), pltpu.VMEM((rows, 1), F32),
                        pltpu.VMEM((rows, MLA_KVLORA), F32),
                        pltpu.SemaphoreType.DMA((2,))])
    return pl.pallas_call(
        functools.partial(_mla_sample_kernel, ts=ts, n_pages=n_pages), grid_spec=grid_spec,
        out_shape=jax.ShapeDtypeStruct((bs, ts, MLA_W), F32),
        compiler_params=_cparams(("arbitrary", "arbitrary")), name="mla_sample",
    )(page_table, q1, q2, pool_t, latn, w_uv_cat)


def _prepare_weights(g_pre, w_in, nsa_w_cmp_k, nsa_w_cmp_v, mla_g_q, mla_w_uq, mla_g_kv, mla_w_uk, mla_w_uv,
                     mem_g, mem_w_kv, w_br_nsa, w_br_mla, w_br_mem, w_out, g_post):
    layers = []
    for l in range(w_in.shape[0]):
        w_row, w_t, w_uq = _front_weights(w_in[l], mla_w_uq[l])
        w_z, w_g, w_br = _back_weights(w_in[l], w_br_nsa[l], w_br_mla[l], w_br_mem[l])
        layers.append(dict(
            g_pre=g_pre[l], w_row=w_row, w_t=w_t, g_q=mla_g_q[l], w_uq=w_uq, g_kv=mla_g_kv[l], w_uk=mla_w_uk[l],
            w_bd=_compress_weights(nsa_w_cmp_k[l], nsa_w_cmp_v[l]), w_uv=mla_w_uv[l].astype(BF16),
            w_past=_past_compress_weights(nsa_w_cmp_k[l], nsa_w_cmp_v[l]),
            w_uv_cat=jnp.transpose(mla_w_uv[l], (1, 0, 2)).reshape(MLA_KVLORA, MLA_W).astype(BF16),
            mem_g=mem_g[l], mem_w_kv=mem_w_kv[l].astype(BF16), w_z=w_z, w_g=w_g, w_br=w_br,
            w_out=w_out[l].astype(BF16), g_post=g_post[l]))
    return layers


def _front_layer(x, pos, p, tm):
    names = ("q", "rc", "bg", "qmem", "qlat", "qrp", "rows_t", "win_t", "kv_t", "lat_t", "latx")
    outs = _front(x, pos, p["g_pre"], p["w_row"], p["w_t"], p["g_q"], p["w_uq"], p["g_kv"], p["w_uk"], tm)
    return dict(zip(names, outs))


def _back_layer(x, o_nsa, o_mla, o_mem, p, tm):
    flat = lambda a: a.reshape(-1, a.shape[-1])
    y = _back(flat(x), flat(o_nsa), flat(o_mla), flat(o_mem), p["g_pre"], p["w_z"], p["w_g"], p["w_br"],
              p["w_out"], p["g_post"], tm)
    return y.reshape(x.shape)


def _prompt_layer(x, mem_prompt, params, l):
    p = params[l]
    b, t, _ = x.shape
    f = _front_layer(x, jnp.arange(t), p, min(256, t))
    nb = t // NSA_BLK
    kcvc = _compress(f["rc"].reshape(b * nb, NSA_BLK * 256), p["w_bd"], NSA_BLK).reshape(b, nb, 256)
    assert nb <= LANE and t % LANE == 0
    kcvc_pad = jnp.pad(kcvc, ((0, 0), (0, LANE - nb), (0, 0)))
    o_nsa = _nsa_prompt(f["q"], f["bg"], kcvc_pad, f["kv_t"], LANE)
    o_mla = _mla_prompt(f["qlat"], f["qrp"], f["latx"], p["w_uv"], min(128, t))
    n_mem = mem_prompt.shape[1]
    mem_kv = _mem_kv(mem_prompt.reshape(b * n_mem, D_MODEL), p["mem_g"], p["mem_w_kv"], min(256, b * n_mem))
    mem_kv = mem_kv.reshape(b, n_mem, 2 * MEM_W)
    o_mem = _mem_attend(f["qmem"], mem_kv, min(256, t))
    y = _back_layer(x, o_nsa, o_mla, o_mem, p, min(256, b * t))
    return dict(f, kcvc=kcvc, o_nsa=o_nsa, o_mla=o_mla, mem_kv=mem_kv, o_mem=o_mem, y=y)


def _sample_layer(x, pool_t, mla_t, win_t, mem_kv, page_table, params, l):
    p = params[l]
    bs, ts, _ = x.shape
    n_tok = bs * ts
    n_pages = page_table.shape[1]
    past_len = n_pages * PAGE_SIZE
    pos = past_len + jnp.arange(n_tok) % ts
    f = _front_layer(x.reshape(1, n_tok, D_MODEL), pos, p, min(256, n_tok))
    past = _past_compress(page_table, pool_t, p["w_past"], min(64, n_pages))
    new = _compress(f["rc"].reshape(bs, ts * 256), p["w_bd"], ts)
    new = new.reshape(bs, 2, NSA_KV, 1, NSA_HD)
    nb = past.shape[3] + 1
    nbp = -(-nb // LANE) * LANE
    kcvc = jnp.concatenate([past, new, jnp.zeros((bs, 2, NSA_KV, nbp - nb, NSA_HD), F32)], axis=3)
    qg = f["q"].reshape(bs, ts, NSA_KV, NSA_REP, NSA_HD).transpose(0, 2, 3, 1, 4).reshape(bs, NSA_KV, -1, NSA_HD)
    gates = f["bg"][0, :, :NSA_HEADS * 3].reshape(bs, ts, NSA_KV, NSA_REP, 3)
    gates = gates.transpose(0, 2, 3, 1, 4).reshape(bs, NSA_KV, -1, 3)
    o_cmp, sel = _cmp_sample(qg, kcvc, ts, past_len, nb, 8)
    n_sel = min(NSA_TOPK, nb)
    sel = sel[:, :, :n_sel].reshape(bs, NSA_KV * ts * n_sel)
    o_nsa, win_new = _slc_sample(page_table, sel, qg, gates, o_cmp, pool_t, f["kv_t"], win_t, f["win_t"],
                                 ts, past_len)
    o_nsa = o_nsa.reshape(bs, NSA_KV, NSA_REP, ts, NSA_HD).transpose(0, 3, 1, 2, 4).reshape(bs, ts, NSA_W)
    q1 = f["qlat"][0].reshape(MLA_HEADS, bs, ts, MLA_KVLORA).transpose(1, 2, 0, 3).reshape(bs, -1, MLA_KVLORA)
    q2 = jnp.pad(f["qrp"].reshape(bs, ts * MLA_HEADS, MLA_ROPE), ((0, 0), (0, 0), (0, LANE - MLA_ROPE)))
    o_mla = _mla_sample(page_table, q1, q2, mla_t, f["latx"], p["w_uv_cat"], ts, min(64, n_pages))
    pad_t = 16
    qmem = jnp.pad(f["qmem"].reshape(bs, ts, MEM_W), ((0, 0), (0, pad_t - ts), (0, 0)))
    o_mem = _mem_attend(qmem, mem_kv, pad_t)[:, :ts]
    y = _back_layer(x, o_nsa, o_mla, o_mem, p, min(256, n_tok))
    return dict(f, kcvc=kcvc, o_cmp=o_cmp, sel=sel, o_nsa=o_nsa, win_new=win_new, o_mla=o_mla, o_mem=o_mem, y=y)


def kernel(x_prompt, x_sample, cache_nsa_kv, cache_mla, cache_win_kv, cache_mem_kv, page_table, mem_prompt,
           g_pre, w_in, nsa_w_cmp_k, nsa_w_cmp_v, mla_g_q, mla_w_uq, mla_g_kv, mla_w_uk, mla_w_uv,
           mem_g, mem_w_kv, w_br_nsa, w_br_mla, w_br_mem, w_out, g_post):
    params = _prepare_weights(g_pre, w_in, nsa_w_cmp_k, nsa_w_cmp_v, mla_g_q, mla_w_uq, mla_g_kv, mla_w_uk,
                              mla_w_uv, mem_g, mem_w_kv, w_br_nsa, w_br_mla, w_br_mem, w_out, g_post)
    depth = w_in.shape[0]
    bp, tp, _ = x_prompt.shape
    bs, ts, _ = x_sample.shape
    n_pool = cache_nsa_kv.shape[1]
    wb = cache_win_kv.shape[2]
    n_mem = mem_prompt.shape[1]
    wp = min(NSA_WINDOW, tp)
    xp, xs = x_prompt, x_sample
    outs = [[] for _ in range(7)]
    for l in range(depth):
        pool_t = jnp.transpose(cache_nsa_kv[l], (0, 2, 3, 4, 1)).reshape(n_pool, 4 * NSA_KV * NSA_HD, PAGE_SIZE)
        mla_t = jnp.transpose(cache_mla[l], (0, 2, 1))
        win_t = jnp.transpose(cache_win_kv[l], (0, 2, 3, 4, 1)).reshape(bs, 2 * NSA_KV * NSA_HD, wb)
        mem_kv_s = cache_mem_kv[l].reshape(bs, n_mem, 2 * MEM_W)
        rp = _prompt_layer(xp, mem_prompt, params, l)
        rs = _sample_layer(xs, pool_t, mla_t, win_t, mem_kv_s, page_table, params, l)
        xp, xs = rp["y"], rs["y"]
        outs[0].append(rp["rows_t"].reshape(bp, 4, NSA_KV, NSA_HD, tp).transpose(0, 4, 1, 2, 3))
        outs[1].append(rs["rows_t"][0].T.reshape(bs, ts, 4, NSA_KV, NSA_HD))
        outs[2].append(rp["lat_t"].transpose(0, 2, 1))
        outs[3].append(rs["lat_t"][0].T.reshape(bs, ts, LATENT))
        outs[4].append(rp["win_t"][:, :, tp - wp:].reshape(bp, 2, NSA_KV, NSA_HD, wp).transpose(0, 4, 1, 2, 3))
        outs[5].append(rs["win_new"].reshape(bs, 2, NSA_KV, NSA_HD, wb).transpose(0, 4, 1, 2, 3))
        outs[6].append(rp["mem_kv"].reshape(bp, n_mem, 2, MEM_HEADS, MEM_HD))
    return (xp, xs) + tuple(jnp.stack(o, axis=0) for o in outs)
```

```python
import functools

import numpy as np
import jax
import jax.numpy as jnp
from jax import lax
from jax.experimental import pallas as pl
from jax.experimental.pallas import tpu as pltpu

F32 = jnp.float32
BF16 = jnp.bfloat16

D_MODEL = 1024
PAGE_SIZE = 128
EPS = 1e-6
ROPE_THETA = 500000.0
NSA_HEADS = 8
NSA_KV = 2
NSA_REP = NSA_HEADS // NSA_KV
NSA_HD = 64
NSA_ROT = NSA_HD // 4
NSA_BLK = 64
NSA_TOPK = 16
NSA_WINDOW = 512
NSA_SCALE = NSA_HD ** -0.5
MLA_HEADS = 8
MLA_QLORA = 384
MLA_KVLORA = 256
MLA_NOPE = 64
MLA_ROPE = 32
MLA_V = 64
MLA_SCALE = (MLA_NOPE + MLA_ROPE) ** -0.5
LATENT = MLA_KVLORA + MLA_ROPE
MEM_HEADS = 4
MEM_HD = 128
MEM_SCALE = MEM_HD ** -0.5
NSA_W = NSA_HEADS * NSA_HD
MLA_W = MLA_HEADS * MLA_V
MEM_W = MEM_HEADS * MEM_HD
N_BRANCH = 3
NSA_KV_W = 3 * 2 * NSA_KV * NSA_HD
IN_SPLITS = (NSA_W, NSA_KV_W, NSA_HEADS * 3, NSA_W,
             MLA_QLORA, MLA_KVLORA, MLA_ROPE, MLA_W,
             MEM_W, MEM_W, N_BRANCH * D_MODEL)
NEG = -1e30
TINY = 1e-30
SEL_BONUS = 1e3
LANE = 128
ROPE_REP = LANE // MLA_ROPE
LATX = MLA_KVLORA + LANE
VMEM_LIMIT = 56 * 1024 * 1024


def _cparams(sem):
    return pltpu.CompilerParams(dimension_semantics=sem, vmem_limit_bytes=VMEM_LIMIT)


def _dot(a, b):
    return jnp.dot(a, b, preferred_element_type=F32)


def _dot_nt(a, b):
    return lax.dot_general(a, b, (((1,), (1,)), ((), ())), preferred_element_type=F32)


def _rms_scale(x):
    return lax.rsqrt(jnp.mean(x * x, axis=-1, keepdims=True) + EPS)


def _softmax_rows(s, mask):
    s = jnp.where(mask, s, NEG)
    p = jnp.where(mask, jnp.exp(s - jnp.max(s, axis=-1, keepdims=True)), 0.0)
    return p / jnp.maximum(jnp.sum(p, axis=-1, keepdims=True), TINY)


def _softmax_rows2(sa, mask_a, sb, mask_b):
    sa = jnp.where(mask_a, sa, NEG)
    sb = jnp.where(mask_b, sb, NEG)
    m = jnp.maximum(jnp.max(sa, axis=-1, keepdims=True), jnp.max(sb, axis=-1, keepdims=True))
    pa = jnp.where(mask_a, jnp.exp(sa - m), 0.0)
    pb = jnp.where(mask_b, jnp.exp(sb - m), 0.0)
    den = jnp.maximum(jnp.sum(pa, axis=-1, keepdims=True) + jnp.sum(pb, axis=-1, keepdims=True), TINY)
    return pa / den, pb / den


def _rope_lanes(x, cs_ref, group, half):
    w = x.shape[1]
    reps = w // LANE
    cos = jnp.concatenate([cs_ref[0]] * reps, axis=1)
    sin = jnp.concatenate([cs_ref[1]] * reps, axis=1)
    lane = lax.broadcasted_iota(jnp.int32, x.shape, 1)
    up = pltpu.roll(x, w - half, 1)
    down = pltpu.roll(x, half, 1)
    rot = jnp.where(lane % group < half, up, down)
    return x * cos + rot * sin


def _front_kernel(x_ref, gpre_ref, wr_ref, wt_ref, gq_ref, wuq_ref, gkv_ref, wuk_ref,
                  csn_ref, csm_ref, tn_ref, tm_ref,
                  q_ref, rc_ref, bg_ref, qmem_ref, qlat_ref, qrp_ref,
                  rows_t_ref, win_t_ref, kv_t_ref, lat_t_ref, latx_ref):
    x = x_ref[0]
    h = (x * _rms_scale(x) * gpre_ref[...]).astype(BF16)
    yr = _dot(h, wr_ref[...])
    q = _rope_lanes(yr[:, 0:NSA_W], csn_ref, NSA_HD, NSA_ROT // 2)
    q_ref[0] = (q * NSA_SCALE).astype(BF16)
    kc = _rope_lanes(yr[:, 512:640], csn_ref, NSA_HD, NSA_ROT // 2)
    rc_ref[0] = jnp.concatenate([kc, yr[:, 640:768]], axis=1).astype(BF16)
    cq = yr[:, 768:1152]
    qmem_ref[0] = yr[:, 1152:1664].astype(BF16)
    bg_ref[0] = jax.nn.sigmoid(yr[:, 1664:1792])
    cqn = (cq * _rms_scale(cq) * gq_ref[...]).astype(BF16)
    qm = _dot(cqn, wuq_ref[...])
    qrp_ref[0] = (_rope_lanes(qm[:, 512:768], csm_ref, MLA_ROPE, MLA_ROPE // 2) * MLA_SCALE).astype(BF16)
    for hd in range(MLA_HEADS):
        qn = qm[:, hd * MLA_NOPE:(hd + 1) * MLA_NOPE].astype(BF16)
        qlat_ref[0, hd] = (_dot(qn, wuk_ref[hd]) * MLA_SCALE).astype(BF16)
    yt = _dot_nt(wt_ref[...], h)
    cn, sn = tn_ref[0], tn_ref[1]
    half = NSA_ROT // 2
    pieces = []
    for base in range(0, NSA_KV_W, NSA_HD):
        is_key = (base // (NSA_KV * NSA_HD)) % 2 == 0
        if is_key:
            x1 = yt[base:base + half]
            x2 = yt[base + half:base + 2 * half]
            pieces += [x1 * cn - x2 * sn, x2 * cn + x1 * sn, yt[base + 2 * half:base + NSA_HD]]
        else:
            pieces.append(yt[base:base + NSA_HD])
    kvt = jnp.concatenate(pieces, axis=0)
    rows_t_ref[0] = kvt[0:512]
    win_t_ref[0] = kvt[512:768]
    kv_t_ref[0] = kvt[256:768].astype(BF16)
    ckv = yt[768:1024]
    ckvn = ckv * lax.rsqrt(jnp.mean(ckv * ckv, axis=0, keepdims=True) + EPS) * gkv_ref[...]
    cm, sm = tm_ref[0], tm_ref[1]
    k1 = yt[1024:1040]
    k2 = yt[1040:1056]
    kr = jnp.concatenate([k1 * cm - k2 * sm, k2 * cm + k1 * sm], axis=0)
    lat_t_ref[0] = jnp.concatenate([ckvn, kr], axis=0)
    krb = kr.astype(BF16)
    latx_ref[0] = jnp.concatenate([ckvn.astype(BF16)] + [krb] * ROPE_REP, axis=0)


def _rope_tables(pos):
    pos = pos.astype(F32)

    def tables(n_rot, group):
        half = n_rot // 2
        inv = ROPE_THETA ** (-jnp.arange(half, dtype=F32) / half)
        ang = pos[:, None] * inv[None, :]
        cos, sin = jnp.cos(ang), jnp.sin(ang)
        t = pos.shape[0]
        pad1 = jnp.ones((t, group - n_rot), F32)
        pad0 = jnp.zeros((t, group - n_rot), F32)
        cos_g = jnp.concatenate([cos, cos, pad1], axis=1)
        sin_g = jnp.concatenate([-sin, sin, pad0], axis=1)
        reps = LANE // group
        row = jnp.stack([jnp.tile(cos_g, (1, reps)), jnp.tile(sin_g, (1, reps))])
        col = jnp.stack([cos.T, sin.T])
        return row, col

    csn, tn = tables(NSA_ROT, NSA_HD)
    csm, tm = tables(MLA_ROPE, MLA_ROPE)
    return csn, csm, tn, tm


def _front_weights(w_in, mla_w_uq):
    o = np.cumsum((0,) + IN_SPLITS)
    w_q, w_kv, w_bg = w_in[:, o[0]:o[1]], w_in[:, o[1]:o[2]], w_in[:, o[2]:o[3]]
    w_cq, w_ckv, w_kr = w_in[:, o[4]:o[5]], w_in[:, o[5]:o[6]], w_in[:, o[6]:o[7]]
    w_qmem = w_in[:, o[8]:o[9]]
    pad = jnp.zeros((D_MODEL, LANE - NSA_HEADS * 3), F32)
    w_row = jnp.concatenate([w_q, w_kv[:, 0:256], w_cq, w_qmem, w_bg, pad], axis=1).astype(BF16)
    w_t = jnp.concatenate([w_kv, w_ckv, w_kr], axis=1).T.astype(BF16)
    wu = mla_w_uq.reshape(MLA_QLORA, MLA_HEADS, MLA_NOPE + MLA_ROPE)
    w_uq = jnp.concatenate([wu[:, :, :MLA_NOPE].reshape(MLA_QLORA, -1),
                            wu[:, :, MLA_NOPE:].reshape(MLA_QLORA, -1)], axis=1).astype(BF16)
    return w_row, w_t, w_uq


def _front(x, pos, g_pre, w_row, w_t, g_q, w_uq, g_kv, w_uk, tm):
    b, t, _ = x.shape
    csn, csm, tn, tmm = _rope_tables(pos)
    nt = t // tm
    tok = lambda w: pl.BlockSpec((1, tm, w), lambda i, j: (i, j, 0))
    tlay = lambda r: pl.BlockSpec((1, r, tm), lambda i, j: (i, 0, j))
    full = lambda a: pl.BlockSpec(a.shape, lambda i, j: (0,) * a.ndim)
    g_pre2, g_q2 = g_pre.reshape(1, -1), g_q.reshape(1, -1)
    g_kv2 = g_kv.reshape(-1, 1)
    w_uk = w_uk.astype(BF16)
    in_specs = [tok(D_MODEL), full(g_pre2), full(w_row), full(w_t), full(g_q2), full(w_uq), full(g_kv2),
                full(w_uk),
                pl.BlockSpec((2, tm, LANE), lambda i, j: (0, j, 0)),
                pl.BlockSpec((2, tm, LANE), lambda i, j: (0, j, 0)),
                pl.BlockSpec((2, NSA_ROT // 2, tm), lambda i, j: (0, 0, j)),
                pl.BlockSpec((2, MLA_ROPE // 2, tm), lambda i, j: (0, 0, j))]
    out_shape = [
        jax.ShapeDtypeStruct((b, t, NSA_W), BF16),
        jax.ShapeDtypeStruct((b, t, 256), BF16),
        jax.ShapeDtypeStruct((b, t, LANE), F32),
        jax.ShapeDtypeStruct((b, t, MEM_W), BF16),
        jax.ShapeDtypeStruct((b, MLA_HEADS, t, MLA_KVLORA), BF16),
        jax.ShapeDtypeStruct((b, t, MLA_HEADS * MLA_ROPE), BF16),
        jax.ShapeDtypeStruct((b, 512, t), F32),
        jax.ShapeDtypeStruct((b, 256, t), F32),
        jax.ShapeDtypeStruct((b, 512, t), BF16),
        jax.ShapeDtypeStruct((b, LATENT, t), F32),
        jax.ShapeDtypeStruct((b, LATX, t), BF16),
    ]
    out_specs = [tok(NSA_W), tok(256), tok(LANE), tok(MEM_W),
                 pl.BlockSpec((1, MLA_HEADS, tm, MLA_KVLORA), lambda i, j: (i, 0, j, 0)),
                 tok(MLA_HEADS * MLA_ROPE),
                 tlay(512), tlay(256), tlay(512), tlay(LATENT), tlay(LATX)]
    return pl.pallas_call(
        _front_kernel, grid=(b, nt), in_specs=in_specs, out_specs=out_specs, out_shape=out_shape,
        compiler_params=_cparams(("parallel", "parallel")), name="front",
    )(x, g_pre2, w_row, w_t, g_q2, w_uq, g_kv2, w_uk, csn, csm, tn, tmm)


def _compress_weights(wk, wv):
    z = jnp.zeros_like(wk)
    rows = [jnp.concatenate(r, axis=2) for r in ([wk, z, z, z], [z, wk, z, z], [z, z, wv, z], [z, z, z, wv])]
    return jnp.concatenate(rows, axis=1).astype(BF16)


def _compress_kernel(x_ref, w_ref, o_ref):
    @pl.when(pl.program_id(0) == 0)
    def _():
        o_ref[...] = jnp.zeros_like(o_ref)

    o_ref[...] += _dot(x_ref[...], w_ref[0])


def _compress(x, w_bd, n_l):
    r = x.shape[0]
    return pl.pallas_call(
        _compress_kernel, grid=(n_l,),
        in_specs=[pl.BlockSpec((r, 256), lambda l: (0, l)),
                  pl.BlockSpec((1, 256, 256), lambda l: (l, 0, 0))],
        out_specs=pl.BlockSpec((r, 256), lambda l: (0, 0)),
        out_shape=jax.ShapeDtypeStruct((r, 256), F32),
        compiler_params=_cparams(("arbitrary",)), name="nsa_compress",
    )(x, w_bd)


def _select_blocks(score, nb, n_sel):
    r = score.shape[0]
    st = score.T[0:nb]
    blk = lax.broadcasted_iota(jnp.int32, (nb, 1), 0)
    rank = jnp.zeros((nb, r), F32)
    for m in range(nb):
        sm = st[m:m + 1]
        rank = rank + jnp.where((sm > st) | ((sm == st) & (m < blk)), 1.0, 0.0)
    sel_t = jnp.where(rank < n_sel, 1.0, 0.0)
    return jnp.concatenate([sel_t, jnp.zeros((LANE - nb, r), F32)], axis=0).T


ATT_KC = 512


def _lane_tiles(x):
    return [x[:, j * LANE:(j + 1) * LANE] for j in range(x.shape[1] // LANE)]


def _tile_max(x):
    return functools.reduce(jnp.maximum, _lane_tiles(x))


def _nsa_prompt_kernel(q_ref, bg_ref, kcvc_ref, kv_ref, o_ref, bias_ref, s_buf, mx_ref, l_ref, acc_ref,
                       *, tq, t_len):
    nb = t_len // NSA_BLK
    kc = min(ATT_KC, t_len)
    q0 = pl.program_id(1) * tq
    n_kc = (q0 + tq + kc - 1) // kc
    qpos = q0 + lax.broadcasted_iota(jnp.int32, (tq, 1), 0)
    blk = lax.broadcasted_iota(jnp.int32, (1, LANE), 1)
    vis = ((blk + 1) * NSA_BLK <= qpos + 1) & (blk < nb)
    cur = qpos // NSA_BLK
    valid = (blk <= cur) & (blk < nb)
    forced = (blk == 0) | (blk == cur) | (blk == cur - 1)
    key = lax.broadcasted_iota(jnp.int32, (1, t_len), 1)
    expand = (lax.broadcasted_iota(jnp.int32, (LANE, t_len), 1) // NSA_BLK
              == lax.broadcasted_iota(jnp.int32, (LANE, t_len), 0)).astype(BF16)
    wlen = min(NSA_WINDOW + tq, t_len)
    wstart = pl.multiple_of(jnp.maximum(q0 - NSA_WINDOW, 0), LANE)
    wdist = qpos - (wstart + lax.broadcasted_iota(jnp.int32, (1, wlen), 1))
    wbias = jnp.where((wdist >= 0) & (wdist < NSA_WINDOW), 0.0, NEG)
    q = q_ref[0]
    gates = bg_ref[0]
    kcvc = kcvc_ref[0].astype(BF16)
    qs = [q[:, h * NSA_HD:(h + 1) * NSA_HD] for h in range(NSA_HEADS)]
    o_cmp = []
    for g in range(NSA_KV):
        kcg = kcvc[:, g * NSA_HD:(g + 1) * NSA_HD]
        vcg = kcvc[:, 128 + g * NSA_HD:128 + (g + 1) * NSA_HD]
        psum = jnp.zeros((tq, LANE), F32)
        for r in range(NSA_REP):
            p = _softmax_rows(_dot_nt(qs[g * NSA_REP + r], kcg), vis)
            o_cmp.append(_dot(p.astype(BF16), vcg))
            psum = psum + p
        score = jnp.where(valid, psum + jnp.where(forced, SEL_BONUS, 0.0), -jnp.inf)
        sel = _select_blocks(score, nb, min(NSA_TOPK, nb))
        bias_ref[g] = jnp.where((_dot(sel.astype(BF16), expand) > 0.5) & (key <= qpos), 0.0, NEG)
    mx_ref[...] = jnp.full(mx_ref.shape, NEG, F32)
    l_ref[...] = jnp.zeros(l_ref.shape, F32)
    acc_ref[...] = jnp.zeros(acc_ref.shape, F32)

    def pass1(c, carry):
        k0 = pl.multiple_of(c * kc, kc)
        for h in range(NSA_HEADS):
            g = h // NSA_REP
            s = _dot(qs[h], kv_ref[0, g * NSA_HD:(g + 1) * NSA_HD, pl.ds(k0, kc)]) + bias_ref[g, :, pl.ds(k0, kc)]
            s_buf[h, :, pl.ds(k0, kc)] = s
            mx_ref[h] = jnp.maximum(mx_ref[h], _tile_max(s))
        return carry

    lax.fori_loop(0, n_kc, pass1, 0)
    for h in range(NSA_HEADS):
        mx_ref[h] = jnp.broadcast_to(jnp.max(mx_ref[h], axis=1, keepdims=True), (tq, LANE))

    def pass2(c, carry):
        k0 = pl.multiple_of(c * kc, kc)
        for h in range(NSA_HEADS):
            g = h // NSA_REP
            m = mx_ref[h]
            ps = [jnp.exp(t - m) for t in _lane_tiles(s_buf[h, :, pl.ds(k0, kc)])]
            l_ref[h] += functools.reduce(jnp.add, ps)
            p = jnp.concatenate(ps, axis=1).astype(BF16)
            acc_ref[h] += _dot_nt(p, kv_ref[0, 128 + g * NSA_HD:128 + (g + 1) * NSA_HD, pl.ds(k0, kc)])
        return carry

    lax.fori_loop(0, n_kc, pass2, 0)
    outs = []
    for h in range(NSA_HEADS):
        g = h // NSA_REP
        o_slc = acc_ref[h] / jnp.sum(l_ref[h], axis=1, keepdims=True)
        kw = kv_ref[0, 256 + g * NSA_HD:256 + (g + 1) * NSA_HD, pl.ds(wstart, wlen)]
        vw = kv_ref[0, 384 + g * NSA_HD:384 + (g + 1) * NSA_HD, pl.ds(wstart, wlen)]
        s = _dot(qs[h], kw) + wbias
        p = jnp.exp(s - jnp.max(s, axis=1, keepdims=True))
        o_win = _dot_nt(p.astype(BF16), vw) / jnp.sum(p, axis=1, keepdims=True)
        outs.append(gates[:, 3 * h:3 * h + 1] * o_cmp[h] + gates[:, 3 * h + 1:3 * h + 2] * o_slc
                    + gates[:, 3 * h + 2:3 * h + 3] * o_win)
    o_ref[0] = jnp.concatenate(outs, axis=1)


def _nsa_prompt(q, bg, kcvc, kv_t, tq):
    b, t, _ = q.shape
    tok = lambda w: pl.BlockSpec((1, tq, w), lambda i, j: (i, j, 0))
    return pl.pallas_call(
        functools.partial(_nsa_prompt_kernel, tq=tq, t_len=t), grid=(b, t // tq),
        in_specs=[tok(NSA_W), tok(LANE),
                  pl.BlockSpec((1, LANE, 256), lambda i, j: (i, 0, 0)),
                  pl.BlockSpec((1, 512, t), lambda i, j: (i, 0, 0))],
        out_specs=tok(NSA_W),
        out_shape=jax.ShapeDtypeStruct((b, t, NSA_W), F32),
        scratch_shapes=[pltpu.VMEM((NSA_KV, tq, t), F32), pltpu.VMEM((NSA_HEADS, tq, t), F32),
                        pltpu.VMEM((NSA_HEADS, tq, LANE), F32), pltpu.VMEM((NSA_HEADS, tq, LANE), F32),
                        pltpu.VMEM((NSA_HEADS, tq, NSA_HD), F32)],
        compiler_params=_cparams(("parallel", "parallel")), name="nsa_prompt",
    )(q, bg, kcvc, kv_t)


def _stack_rope_queries(qrp):
    r = qrp.shape[0]
    slot = lax.broadcasted_iota(jnp.int32, (r, LANE), 1) // MLA_ROPE
    parts = []
    for h in range(MLA_HEADS):
        tile = qrp[:, (h // ROPE_REP) * LANE:(h // ROPE_REP + 1) * LANE]
        parts.append(jnp.where(slot == h % ROPE_REP, tile, jnp.zeros_like(tile)))
    return jnp.concatenate(parts, axis=0)


def _mla_prompt_kernel(qlat_ref, qrp_ref, latx_ref, wuv_ref, o_ref, s_buf, mx_ref, l_ref, acc_ref, *, tq, t_len):
    kc = min(ATT_KC, t_len)
    rows = MLA_HEADS * tq
    q0 = pl.program_id(1) * tq
    n_full = q0 // kc
    q1 = qlat_ref[0].reshape(rows, MLA_KVLORA)
    q2 = _stack_rope_queries(qrp_ref[0])

    def scores(c):
        k0 = pl.multiple_of(c * kc, kc)
        return (_dot(q1, latx_ref[0, 0:MLA_KVLORA, pl.ds(k0, kc)])
                + _dot(q2, latx_ref[0, MLA_KVLORA:, pl.ds(k0, kc)]))

    mx_ref[...] = jnp.full(mx_ref.shape, NEG, F32)
    l_ref[...] = jnp.zeros(l_ref.shape, F32)
    acc_ref[...] = jnp.zeros(acc_ref.shape, F32)

    def pass1(c, carry):
        s = scores(c)
        s_buf[:, pl.ds(pl.multiple_of(c * kc, kc), kc)] = s
        mx_ref[...] = jnp.maximum(mx_ref[...], _tile_max(s))
        return carry

    lax.fori_loop(0, n_full, pass1, 0)
    qpos = q0 + lax.broadcasted_iota(jnp.int32, (rows, 1), 0) % tq
    key = n_full * kc + lax.broadcasted_iota(jnp.int32, (1, kc), 1)
    s = scores(n_full) + jnp.where(key <= qpos, 0.0, NEG)
    s_buf[:, pl.ds(pl.multiple_of(n_full * kc, kc), kc)] = s
    m = jnp.max(jnp.maximum(mx_ref[...], _tile_max(s)), axis=1, keepdims=True)
    mx_ref[...] = jnp.broadcast_to(m, (rows, LANE))

    def pass2(c, carry):
        k0 = pl.multiple_of(c * kc, kc)
        mm = mx_ref[...]
        ps = [jnp.exp(t - mm) for t in _lane_tiles(s_buf[:, pl.ds(k0, kc)])]
        l_ref[...] += functools.reduce(jnp.add, ps)
        p = jnp.concatenate(ps, axis=1).astype(BF16)
        acc_ref[...] += _dot_nt(p, latx_ref[0, 0:MLA_KVLORA, pl.ds(k0, kc)])
        return carry

    lax.fori_loop(0, n_full + 1, pass2, 0)
    o_lat = (acc_ref[...] / jnp.sum(l_ref[...], axis=1, keepdims=True)).astype(BF16)
    o_ref[0] = jnp.concatenate([_dot(o_lat[h * tq:(h + 1) * tq], wuv_ref[h]) for h in range(MLA_HEADS)], axis=1)


def _mla_prompt(qlat, qrp, latx, w_uv, tq):
    b, _, t, _ = qlat.shape
    return pl.pallas_call(
        functools.partial(_mla_prompt_kernel, tq=tq, t_len=t), grid=(b, t // tq),
        in_specs=[pl.BlockSpec((1, MLA_HEADS, tq, MLA_KVLORA), lambda i, j: (i, 0, j, 0)),
                  pl.BlockSpec((1, tq, 256), lambda i, j: (i, j, 0)),
                  pl.BlockSpec((1, LATX, t), lambda i, j: (i, 0, 0)),
                  pl.BlockSpec(w_uv.shape, lambda i, j: (0, 0, 0))],
        out_specs=pl.BlockSpec((1, tq, MLA_W), lambda i, j: (i, j, 0)),
        out_shape=jax.ShapeDtypeStruct((b, t, MLA_W), F32),
        scratch_shapes=[pltpu.VMEM((MLA_HEADS * tq, t), F32), pltpu.VMEM((MLA_HEADS * tq, LANE), F32),
                        pltpu.VMEM((MLA_HEADS * tq, LANE), F32), pltpu.VMEM((MLA_HEADS * tq, MLA_KVLORA), F32)],
        compiler_params=_cparams(("parallel", "parallel")), name="mla_prompt",
    )(qlat, qrp, latx, w_uv)


def _mem_kv_kernel(m_ref, g_ref, w_ref, o_ref):
    m = m_ref[...]
    o_ref[...] = _dot((m * _rms_scale(m) * g_ref[...]).astype(BF16), w_ref[...])


def _mem_kv(mem, g, w, tm):
    n = mem.shape[0]
    g2 = g.reshape(1, -1)
    return pl.pallas_call(
        _mem_kv_kernel, grid=(n // tm,),
        in_specs=[pl.BlockSpec((tm, D_MODEL), lambda i: (i, 0)),
                  pl.BlockSpec(g2.shape, lambda i: (0, 0)),
                  pl.BlockSpec(w.shape, lambda i: (0, 0))],
        out_specs=pl.BlockSpec((tm, 2 * MEM_W), lambda i: (i, 0)),
        out_shape=jax.ShapeDtypeStruct((n, 2 * MEM_W), F32),
        compiler_params=_cparams(("parallel",)), name="mem_kv",
    )(mem, g2, w)


def _mem_attend_kernel(q_ref, kv_ref, o_ref):
    q = q_ref[0]
    outs = []
    for h in range(MEM_HEADS):
        k = kv_ref[0, :, h * MEM_HD:(h + 1) * MEM_HD].astype(BF16)
        v = kv_ref[0, :, MEM_W + h * MEM_HD:MEM_W + (h + 1) * MEM_HD].astype(BF16)
        s = _dot_nt(q[:, h * MEM_HD:(h + 1) * MEM_HD], k) * MEM_SCALE
        e = jnp.exp(s - jnp.max(s, axis=-1, keepdims=True))
        p = e / jnp.sum(e, axis=-1, keepdims=True)
        outs.append(_dot(p.astype(BF16), v))
    o_ref[0] = jnp.concatenate(outs, axis=1)


def _mem_attend(q, kv, tq):
    b, t, _ = q.shape
    return pl.pallas_call(
        _mem_attend_kernel, grid=(b, t // tq),
        in_specs=[pl.BlockSpec((1, tq, MEM_W), lambda i, j: (i, j, 0)),
                  pl.BlockSpec((1,) + kv.shape[1:], lambda i, j: (i, 0, 0))],
        out_specs=pl.BlockSpec((1, tq, MEM_W), lambda i, j: (i, j, 0)),
        out_shape=jax.ShapeDtypeStruct((b, t, MEM_W), F32),
        compiler_params=_cparams(("parallel", "parallel")), name="mem_attend",
    )(q, kv)


def _mem_attend_cache_kernel(q_ref, kv_ref, o_ref):
    n_mem = kv_ref.shape[1] // (2 * MEM_HEADS)
    for bb in range(q_ref.shape[0]):
        q = q_ref[bb]
        outs = []
        for h in range(MEM_HEADS):
            k = kv_ref[bb, pl.ds(h, n_mem, stride=2 * MEM_HEADS), :].astype(BF16)
            v = kv_ref[bb, pl.ds(MEM_HEADS + h, n_mem, stride=2 * MEM_HEADS), :].astype(BF16)
            s = _dot_nt(q[:, h * MEM_HD:(h + 1) * MEM_HD], k) * MEM_SCALE
            e = jnp.exp(s - jnp.max(s, axis=-1, keepdims=True))
            p = e / jnp.sum(e, axis=-1, keepdims=True)
            outs.append(_dot(p.astype(BF16), v))
        o_ref[bb] = jnp.concatenate(outs, axis=1)


def _mem_attend_cache(q, kv, n_b):
    b, t, _ = q.shape
    return pl.pallas_call(
        _mem_attend_cache_kernel, grid=(b // n_b,),
        in_specs=[pl.BlockSpec((n_b, t, MEM_W), lambda i: (i, 0, 0)),
                  pl.BlockSpec((n_b,) + kv.shape[1:], lambda i: (i, 0, 0))],
        out_specs=pl.BlockSpec((n_b, t, MEM_W), lambda i: (i, 0, 0)),
        out_shape=jax.ShapeDtypeStruct((b, t, MEM_W), F32),
        compiler_params=_cparams(("parallel",)), name="mem_attend_cache",
    )(q, kv)


def _back_kernel(x_ref, on_ref, om_ref, oe_ref, gpre_ref, wz_ref, wg_ref, wbr_ref, wout_ref, gpost_ref, y_ref):
    x = x_ref[...]
    h = (x * _rms_scale(x) * gpre_ref[...]).astype(BF16)
    y = jnp.zeros(x.shape, F32)
    for j, o_ref in enumerate((on_ref, om_ref, oe_ref)):
        z = _dot(h, wz_ref[j])
        u = (o_ref[...] * (z * jax.nn.sigmoid(z))).astype(BF16)
        gate = jax.nn.sigmoid(_dot(h, wg_ref[j]))
        y = y + gate * _dot(u, wbr_ref[j])
    v = _dot(y.astype(BF16), wout_ref[...])
    y_ref[...] = x + v * _rms_scale(v) * gpost_ref[...]


def _back_weights(w_in, w_br_nsa, w_br_mla, w_br_mem):
    o = np.cumsum((0,) + IN_SPLITS)
    w_z = jnp.stack([w_in[:, o[3]:o[4]], w_in[:, o[7]:o[8]], w_in[:, o[9]:o[10]]]).astype(BF16)
    w_g = jnp.stack([w_in[:, o[10] + j * D_MODEL:o[10] + (j + 1) * D_MODEL] for j in range(N_BRANCH)]).astype(BF16)
    w_br = jnp.stack([w_br_nsa, w_br_mla, w_br_mem]).astype(BF16)
    return w_z, w_g, w_br


def _back(x, o_nsa, o_mla, o_mem, g_pre, w_z, w_g, w_br, w_out, g_post, tm):
    n = x.shape[0]
    g_pre2, g_post2 = g_pre.reshape(1, -1), g_post.reshape(1, -1)
    full = lambda a: pl.BlockSpec(a.shape, lambda i: (0,) * a.ndim)
    tok = lambda w: pl.BlockSpec((tm, w), lambda i: (i, 0))
    return pl.pallas_call(
        _back_kernel, grid=(n // tm,),
        in_specs=[tok(D_MODEL), tok(512), tok(512), tok(512), full(g_pre2), full(w_z), full(w_g), full(w_br),
                  full(w_out), full(g_post2)],
        out_specs=tok(D_MODEL),
        out_shape=jax.ShapeDtypeStruct((n, D_MODEL), F32),
        compiler_params=_cparams(("parallel",)), name="back",
    )(x, o_nsa, o_mla, o_mem, g_pre2, w_z, w_g, w_br, w_out, g_post2)


def _div(a, b):
    return lax.div(a, jnp.int32(b))


def _rem(a, b):
    return lax.rem(a, jnp.int32(b))


def _pipelined_gather(step, n_steps, n_copies, make_copies):
    slot = _rem(step, 2)

    def run(s, sl, start):
        def body(i, carry):
            for cp in make_copies(s, i, sl):
                if start:
                    cp.start()
                else:
                    cp.wait()
            return carry
        lax.fori_loop(0, n_copies, body, 0)

    @pl.when(step == 0)
    def _():
        run(step, slot, True)

    @pl.when(step + 1 < n_steps)
    def _():
        run(step + 1, 1 - slot, True)

    run(step, slot, False)
    return slot


def _past_compress_weights(wk, wv):
    def one(w):
        return jnp.einsum('jk,lde->djlke', jnp.eye(2, dtype=F32), w).reshape(NSA_HD * PAGE_SIZE, 128)
    return jnp.stack([one(wk), one(wv)]).astype(BF16)


def _past_compress_kernel(pt_ref, pool_ref, w_ref, o_ref, buf, a_buf, sem, *, n_pages):
    chunk = buf.shape[2]
    steps_per_b = n_pages // chunk
    step = pl.program_id(0) * steps_per_b + pl.program_id(1)

    def page_copy(s, i, slot):
        page = pt_ref[_div(s, steps_per_b), _rem(s, steps_per_b) * chunk + i]
        return (pltpu.make_async_copy(pool_ref.at[page, pl.ds(0, 256), :], buf.at[slot, :, i, :], sem.at[slot]),)

    slot = _pipelined_gather(step, pl.num_programs(0) * steps_per_b, chunk, page_copy)

    for kv in range(2):
        for g in range(NSA_KV):
            for d in range(NSA_HD):
                a_buf[g * chunk:(g + 1) * chunk, d * PAGE_SIZE:(d + 1) * PAGE_SIZE] = (
                    buf[slot, kv * 128 + g * NSA_HD + d].astype(BF16))
        acc = _dot(a_buf[...], w_ref[kv])
        for g in range(NSA_KV):
            o_ref[0, kv, g] = acc[g * chunk:(g + 1) * chunk]


def _past_compress(page_table, pool_t, w2, chunk):
    bs, n_pages = page_table.shape
    grid_spec = pltpu.PrefetchScalarGridSpec(
        num_scalar_prefetch=1, grid=(bs, n_pages // chunk),
        in_specs=[pl.BlockSpec(memory_space=pl.ANY),
                  pl.BlockSpec(w2.shape, lambda b, c, pt: (0, 0, 0))],
        out_specs=pl.BlockSpec((1, 2, NSA_KV, chunk, 128), lambda b, c, pt: (b, 0, 0, c, 0)),
        scratch_shapes=[pltpu.VMEM((2, 256, chunk, PAGE_SIZE), F32),
                        pltpu.VMEM((NSA_KV * chunk, NSA_HD * PAGE_SIZE), BF16),
                        pltpu.SemaphoreType.DMA((2,))])
    out = pl.pallas_call(
        functools.partial(_past_compress_kernel, n_pages=n_pages), grid_spec=grid_spec,
        out_shape=jax.ShapeDtypeStruct((bs, 2, NSA_KV, n_pages, 128), F32),
        compiler_params=_cparams(("arbitrary", "arbitrary")), name="past_compress",
    )(page_table, pool_t, w2)
    return out.reshape(bs, 2, NSA_KV, n_pages * (PAGE_SIZE // NSA_BLK), NSA_HD)


def _cmp_sample_kernel(q_ref, kcvc_ref, oc_ref, sel_ref, sc_ref, *, ts, past_len, nb):
    nbp = kcvc_ref.shape[3]
    n_b = q_ref.shape[0]
    rows = NSA_REP * ts
    t_row = lax.broadcasted_iota(jnp.int32, (rows, 1), 0) % ts
    blk = lax.broadcasted_iota(jnp.int32, (1, nbp), 1)
    vis = ((blk + 1) * NSA_BLK <= past_len + t_row + 1) & (blk < nb)
    for bb in range(n_b):
        for g in range(NSA_KV):
            kc = kcvc_ref[bb, 0, g].astype(BF16)
            vc = kcvc_ref[bb, 1, g].astype(BF16)
            p = _softmax_rows(_dot_nt(q_ref[bb, g], kc), vis)
            oc_ref[bb, g] = _dot(p.astype(BF16), vc)
            psum = p[0:ts]
            for r in range(1, NSA_REP):
                psum = psum + p[r * ts:(r + 1) * ts]
            sc_ref[(bb * NSA_KV + g) * ts:(bb * NSA_KV + g + 1) * ts, :] = psum
    n_rows = n_b * NSA_KV * ts
    t_sel = lax.broadcasted_iota(jnp.int32, (n_rows, 1), 0) % ts
    cur = (past_len + t_sel) // NSA_BLK
    valid = (blk <= cur) & (blk < nb)
    forced = (blk == 0) | (blk == cur) | (blk == cur - 1)
    score = jnp.where(valid, sc_ref[...] + jnp.where(forced, SEL_BONUS, 0.0), -jnp.inf)
    blk_f = blk.astype(F32)
    taken = jnp.broadcast_to(blk >= nb, score.shape)
    col = lax.broadcasted_iota(jnp.int32, (n_rows, LANE), 1)
    sel = jnp.zeros((n_rows, LANE), F32)
    for k in range(min(NSA_TOPK, nb)):
        best = jnp.max(jnp.where(taken, -jnp.inf, score), axis=1, keepdims=True)
        idx = jnp.min(jnp.where((~taken) & (score == best), blk_f, float(nbp)), axis=1, keepdims=True)
        sel = jnp.where(col == k, idx, sel)
        taken = taken | (blk_f == idx)
    sel_ref[...] = sel.astype(jnp.int32).reshape(sel_ref.shape)


def _cmp_sample(qg, kcvc, ts, past_len, nb, n_b):
    bs = qg.shape[0]
    rows = NSA_REP * ts
    nbp = kcvc.shape[3]
    return pl.pallas_call(
        functools.partial(_cmp_sample_kernel, ts=ts, past_len=past_len, nb=nb), grid=(bs // n_b,),
        in_specs=[pl.BlockSpec((n_b, NSA_KV, rows, NSA_HD), lambda b: (b, 0, 0, 0)),
                  pl.BlockSpec((n_b, 2, NSA_KV, nbp, NSA_HD), lambda b: (b, 0, 0, 0, 0))],
        out_specs=[pl.BlockSpec((n_b, NSA_KV, rows, NSA_HD), lambda b: (b, 0, 0, 0)),
                   pl.BlockSpec((n_b, NSA_KV * ts, LANE), lambda b: (b, 0, 0))],
        out_shape=[jax.ShapeDtypeStruct((bs, NSA_KV, rows, NSA_HD), F32),
                   jax.ShapeDtypeStruct((bs, NSA_KV * ts, LANE), jnp.int32)],
        scratch_shapes=[pltpu.VMEM((n_b * NSA_KV * ts, nbp), F32)],
        compiler_params=_cparams(("parallel",)), name="cmp_sample",
    )(qg, kcvc)


KV_SPAN = (NSA_KV + 1) * NSA_HD


def _slc_sample_kernel(pt_ref, sel_ref, q_ref, gate_ref, oc_ref, pool_ref, kvn_ref, win_ref, winn_ref,
                       o_ref, wout_ref, kvbuf, sem, *, ts, past_len, n_sel):
    b = pl.program_id(0)
    nbs = pl.num_programs(0)
    bpp = PAGE_SIZE // NSA_BLK
    nbp = pt_ref.shape[1] * bpp
    n_tg = NSA_KV * ts
    rows = NSA_REP * ts
    per_b = LANE // ts

    def copies(bb, gt, slot):
        row0 = pl.multiple_of(256 + _div(gt, ts) * NSA_HD, NSA_HD)
        out = []
        for k in range(n_sel):
            s = jnp.minimum(sel_ref[bb, gt * n_sel + k], nbp - 1)
            page = pt_ref[bb, _div(s, bpp)]
            out.append(pltpu.make_async_copy(pool_ref.at[page, pl.ds(row0, KV_SPAN), :],
                                             kvbuf.at[slot, gt, :, pl.ds(k * PAGE_SIZE, PAGE_SIZE)], sem.at[slot]))
        return out

    slot = _pipelined_gather(b, nbs, n_tg, copies)

    t_row = lax.broadcasted_iota(jnp.int32, (rows, 1), 0) % ts
    lane_p = lax.broadcasted_iota(jnp.int32, (1, n_sel * PAGE_SIZE), 1)
    lane_n = lax.broadcasted_iota(jnp.int32, (1, LANE), 1)
    mine = lane_n // ts == b % per_b
    t_key = lane_n % ts
    wb = win_ref.shape[2]
    lane_w = lax.broadcasted_iota(jnp.int32, (1, wb), 1)
    wdist = t_row + wb - lane_w
    wmask = (wdist >= 0) & (wdist < NSA_WINDOW)
    wdist_n = t_row - t_key
    wmask_n = mine & (wdist_n >= 0) & (wdist_n < NSA_WINDOW)
    for g in range(NSA_KV):
        q = q_ref[0, g]
        gates = gate_ref[0, g]
        kn = kvn_ref[0, g * NSA_HD:(g + 1) * NSA_HD, :]
        vn = kvn_ref[0, 128 + g * NSA_HD:128 + (g + 1) * NSA_HD, :]
        s_new = _dot(q, kn)
        o_slc = jnp.zeros((rows, NSA_HD), F32)
        for t in range(ts):
            selv = jnp.zeros((1, n_sel * PAGE_SIZE), jnp.int32)
            chosen = jnp.zeros((1, LANE), jnp.bool_)
            for k in range(n_sel):
                s_k = sel_ref[b, (g * ts + t) * n_sel + k]
                selv = jnp.where(lane_p // PAGE_SIZE == k, s_k, selv)
                chosen = chosen | ((past_len + t_key) // NSA_BLK == s_k)
            pmask = (selv < nbp) & ((lane_p % PAGE_SIZE) // NSA_BLK == selv % bpp) & (t_row == t)
            nmask = mine & chosen & (t_key <= t_row) & (t_row == t)
            kt = kvbuf[slot, g * ts + t, 0:NSA_HD].astype(BF16)
            vt = kvbuf[slot, g * ts + t, KV_SPAN - NSA_HD:].astype(BF16)
            pa, pb = _softmax_rows2(_dot(q, kt), pmask, s_new, nmask)
            o_slc = o_slc + _dot_nt(pa.astype(BF16), vt) + _dot_nt(pb.astype(BF16), vn)
        kw = win_ref[0, g * NSA_HD:(g + 1) * NSA_HD, :].astype(BF16)
        vw = win_ref[0, 128 + g * NSA_HD:128 + (g + 1) * NSA_HD, :].astype(BF16)
        kwn = kvn_ref[0, 256 + g * NSA_HD:256 + (g + 1) * NSA_HD, :]
        vwn = kvn_ref[0, 384 + g * NSA_HD:384 + (g + 1) * NSA_HD, :]
        pa, pb = _softmax_rows2(_dot(q, kw), wmask, _dot(q, kwn), wmask_n)
        o_win = _dot_nt(pa.astype(BF16), vw) + _dot_nt(pb.astype(BF16), vwn)
        o_ref[0, g] = gates[:, 0:1] * oc_ref[0, g] + gates[:, 1:2] * o_slc + gates[:, 2:3] * o_win
    shifted = pltpu.roll(win_ref[0], wb - ts, 1)
    wout_ref[0] = shifted
    new_cols = pltpu.roll(winn_ref[0], (LANE - ts - (b % per_b) * ts) % LANE, 1)
    wout_ref[0, :, wb - LANE:] = jnp.where(lane_n >= LANE - ts, new_cols, shifted[:, wb - LANE:])


def _slc_sample(page_table, sel, qg, gates, o_cmp, pool_t, kvn_t, win_t, winn_t, ts, past_len):
    bs = qg.shape[0]
    rows = NSA_REP * ts
    n_sel = sel.shape[1] // (NSA_KV * ts)
    wb = win_t.shape[2]
    per_b = LANE // ts
    qspec = lambda w: pl.BlockSpec((1, NSA_KV, rows, w), lambda b, pt, sl: (b, 0, 0, 0))
    grid_spec = pltpu.PrefetchScalarGridSpec(
        num_scalar_prefetch=2, grid=(bs,),
        in_specs=[qspec(NSA_HD), qspec(3), qspec(NSA_HD),
                  pl.BlockSpec(memory_space=pl.ANY),
                  pl.BlockSpec((1, 512, LANE), lambda b, pt, sl: (0, 0, b // per_b)),
                  pl.BlockSpec((1, 256, wb), lambda b, pt, sl: (b, 0, 0)),
                  pl.BlockSpec((1, 256, LANE), lambda b, pt, sl: (0, 0, b // per_b))],
        out_specs=[qspec(NSA_HD), pl.BlockSpec((1, 256, wb), lambda b, pt, sl: (b, 0, 0))],
        scratch_shapes=[pltpu.VMEM((2, NSA_KV * ts, KV_SPAN, n_sel * PAGE_SIZE), F32),
                        pltpu.SemaphoreType.DMA((2,))])
    return pl.pallas_call(
        functools.partial(_slc_sample_kernel, ts=ts, past_len=past_len, n_sel=n_sel), grid_spec=grid_spec,
        out_shape=[jax.ShapeDtypeStruct((bs, NSA_KV, rows, NSA_HD), F32),
                   jax.ShapeDtypeStruct((bs, 256, wb), F32)],
        compiler_params=_cparams(("arbitrary",)), name="slc_sample",
    )(page_table, sel, qg, gates, o_cmp, pool_t, kvn_t, win_t, winn_t)


def _mla_sample_kernel(pt_ref, q1_ref, q2_ref, pool_ref, latn_ref, wuv_ref, o_ref,
                       buf, kb, m_ref, l_ref, acc_ref, sem, *, ts, n_pages):
    n_chunk = buf.shape[2] // PAGE_SIZE
    steps_per_b = n_pages // n_chunk
    b, c = pl.program_id(0), pl.program_id(1)
    step = b * steps_per_b + c
    rows = MLA_HEADS * ts

    def page_copy(s, i, slot):
        page = pt_ref[_div(s, steps_per_b), _rem(s, steps_per_b) * n_chunk + i]
        dst = buf.at[slot, :, pl.ds(pl.multiple_of(i * PAGE_SIZE, PAGE_SIZE), PAGE_SIZE)]
        return (pltpu.make_async_copy(pool_ref.at[page], dst, sem.at[slot]),)

    slot = _pipelined_gather(step, pl.num_programs(0) * steps_per_b, n_chunk, page_copy)

    @pl.when(step == 0)
    def _():
        kb[LATENT:, :] = jnp.zeros((LATX - LATENT, kb.shape[1]), BF16)

    @pl.when(c == 0)
    def _():
        m_ref[...] = jnp.full(m_ref.shape, NEG, F32)
        l_ref[...] = jnp.zeros(l_ref.shape, F32)
        acc_ref[...] = jnp.zeros(acc_ref.shape, F32)

    q1 = q1_ref[0]
    q2 = q2_ref[0]

    def update(s, values):
        m_old = m_ref[...]
        m_new = jnp.maximum(m_old, jnp.max(s, axis=-1, keepdims=True))
        alpha = jnp.exp(m_old - m_new)
        p = jnp.exp(s - m_new)
        m_ref[...] = m_new
        l_ref[...] = l_ref[...] * alpha + jnp.sum(p, axis=-1, keepdims=True)
        acc_ref[...] = acc_ref[...] * alpha + _dot_nt(p.astype(BF16), values)

    kb[0:LATENT, :] = buf[slot].astype(BF16)
    update(_dot(q1, kb[0:MLA_KVLORA, :]) + _dot(q2, kb[MLA_KVLORA:, :]), kb[0:MLA_KVLORA, :])

    @pl.when(c == steps_per_b - 1)
    def _():
        latn = latn_ref[0]
        per_b = LANE // ts
        lane_n = lax.broadcasted_iota(jnp.int32, (1, LANE), 1)
        t_row = lax.broadcasted_iota(jnp.int32, (rows, 1), 0) // MLA_HEADS
        mask = (lane_n // ts == b % per_b) & (lane_n % ts <= t_row)
        s = _dot(q1, latn[0:MLA_KVLORA]) + _dot(q2, latn[MLA_KVLORA:])
        update(jnp.where(mask, s, NEG), latn[0:MLA_KVLORA])
        o_lat = (acc_ref[...] / l_ref[...]).astype(BF16)
        full = _dot(o_lat, wuv_ref[...])
        head_row = lax.broadcasted_iota(jnp.int32, full.shape, 0) % MLA_HEADS
        head_lane = lax.broadcasted_iota(jnp.int32, full.shape, 1) // MLA_V
        full = jnp.where(head_row == head_lane, full, 0.0)
        o_ref[0] = jnp.sum(full.reshape(ts, MLA_HEADS, MLA_W), axis=1)


def _mla_sample(page_table, q1, q2, pool_t, latn, w_uv_cat, ts, n_chunk):
    bs, n_pages = page_table.shape
    rows = MLA_HEADS * ts
    per_b = LANE // ts
    grid_spec = pltpu.PrefetchScalarGridSpec(
        num_scalar_prefetch=1, grid=(bs, n_pages // n_chunk),
        in_specs=[pl.BlockSpec((1, rows, MLA_KVLORA), lambda b, c, pt: (b, 0, 0)),
                  pl.BlockSpec((1, rows, LANE), lambda b, c, pt: (b, 0, 0)),
                  pl.BlockSpec(memory_space=pl.ANY),
                  pl.BlockSpec((1, LATX, LANE), lambda b, c, pt: (0, 0, b // per_b)),
                  pl.BlockSpec(w_uv_cat.shape, lambda b, c, pt: (0, 0))],
        out_specs=pl.BlockSpec((1, ts, MLA_W), lambda b, c, pt: (b, 0, 0)),
        scratch_shapes=[pltpu.VMEM((2, LATENT, n_chunk * PAGE_SIZE), F32),
                        pltpu.VMEM((LATX, n_chunk * PAGE_SIZE), BF16),
                        pltpu.VMEM((rows, 1), F32), pltpu.VMEM((rows, 1), F32),
                        pltpu.VMEM((rows, MLA_KVLORA), F32),
                        pltpu.SemaphoreType.DMA((2,))])
    return pl.pallas_call(
        functools.partial(_mla_sample_kernel, ts=ts, n_pages=n_pages), grid_spec=grid_spec,
        out_shape=jax.ShapeDtypeStruct((bs, ts, MLA_W), F32),
        compiler_params=_cparams(("arbitrary", "arbitrary")), name="mla_sample",
    )(page_table, q1, q2, pool_t, latn, w_uv_cat)


def _prepare_weights(g_pre, w_in, nsa_w_cmp_k, nsa_w_cmp_v, mla_g_q, mla_w_uq, mla_g_kv, mla_w_uk, mla_w_uv,
                     mem_g, mem_w_kv, w_br_nsa, w_br_mla, w_br_mem, w_out, g_post):
    layers = []
    for l in range(w_in.shape[0]):
        w_row, w_t, w_uq = _front_weights(w_in[l], mla_w_uq[l])
        w_z, w_g, w_br = _back_weights(w_in[l], w_br_nsa[l], w_br_mla[l], w_br_mem[l])
        layers.append(dict(
            g_pre=g_pre[l], w_row=w_row, w_t=w_t, g_q=mla_g_q[l], w_uq=w_uq, g_kv=mla_g_kv[l], w_uk=mla_w_uk[l],
            w_bd=_compress_weights(nsa_w_cmp_k[l], nsa_w_cmp_v[l]), w_uv=mla_w_uv[l].astype(BF16),
            w_past=_past_compress_weights(nsa_w_cmp_k[l], nsa_w_cmp_v[l]),
            w_uv_cat=jnp.transpose(mla_w_uv[l], (1, 0, 2)).reshape(MLA_KVLORA, MLA_W).astype(BF16),
            mem_g=mem_g[l], mem_w_kv=mem_w_kv[l].astype(BF16), w_z=w_z, w_g=w_g, w_br=w_br,
            w_out=w_out[l].astype(BF16), g_post=g_post[l]))
    return layers


def _front_layer(x, pos, p, tm):
    names = ("q", "rc", "bg", "qmem", "qlat", "qrp", "rows_t", "win_t", "kv_t", "lat_t", "latx")
    outs = _front(x, pos, p["g_pre"], p["w_row"], p["w_t"], p["g_q"], p["w_uq"], p["g_kv"], p["w_uk"], tm)
    return dict(zip(names, outs))


def _back_layer(x, o_nsa, o_mla, o_mem, p, tm):
    flat = lambda a: a.reshape(-1, a.shape[-1])
    y = _back(flat(x), flat(o_nsa), flat(o_mla), flat(o_mem), p["g_pre"], p["w_z"], p["w_g"], p["w_br"],
              p["w_out"], p["g_post"], tm)
    return y.reshape(x.shape)


def _prompt_layer(x, mem_prompt, params, l):
    p = params[l]
    b, t, _ = x.shape
    f = _front_layer(x, jnp.arange(t), p, min(256, t))
    nb = t // NSA_BLK
    kcvc = _compress(f["rc"].reshape(b * nb, NSA_BLK * 256), p["w_bd"], NSA_BLK).reshape(b, nb, 256)
    assert nb <= LANE and t % LANE == 0
    kcvc_pad = jnp.pad(kcvc, ((0, 0), (0, LANE - nb), (0, 0)))
    o_nsa = _nsa_prompt(f["q"], f["bg"], kcvc_pad, f["kv_t"], min(256, t))
    o_mla = _mla_prompt(f["qlat"], f["qrp"], f["latx"], p["w_uv"], min(128, t))
    n_mem = mem_prompt.shape[1]
    mem_kv = _mem_kv(mem_prompt.reshape(b * n_mem, D_MODEL), p["mem_g"], p["mem_w_kv"], min(256, b * n_mem))
    mem_kv = mem_kv.reshape(b, n_mem, 2 * MEM_W)
    o_mem = _mem_attend(f["qmem"], mem_kv, min(256, t))
    y = _back_layer(x, o_nsa, o_mla, o_mem, p, min(256, b * t))
    return dict(f, kcvc=kcvc, o_nsa=o_nsa, o_mla=o_mla, mem_kv=mem_kv, o_mem=o_mem, y=y)


def _sample_layer(x, pool_t, mla_t, win_t, mem_kv, page_table, params, l):
    p = params[l]
    bs, ts, _ = x.shape
    n_tok = bs * ts
    n_pages = page_table.shape[1]
    past_len = n_pages * PAGE_SIZE
    pos = past_len + jnp.arange(n_tok) % ts
    f = _front_layer(x.reshape(1, n_tok, D_MODEL), pos, p, min(256, n_tok))
    past = _past_compress(page_table, pool_t, p["w_past"], min(64, n_pages))
    new = _compress(f["rc"].reshape(bs, ts * 256), p["w_bd"], ts)
    new = new.reshape(bs, 2, NSA_KV, 1, NSA_HD)
    nb = past.shape[3] + 1
    nbp = -(-nb // LANE) * LANE
    kcvc = jnp.concatenate([past, new, jnp.zeros((bs, 2, NSA_KV, nbp - nb, NSA_HD), F32)], axis=3)
    qg = f["q"].reshape(bs, ts, NSA_KV, NSA_REP, NSA_HD).transpose(0, 2, 3, 1, 4).reshape(bs, NSA_KV, -1, NSA_HD)
    gates = f["bg"][0, :, :NSA_HEADS * 3].reshape(bs, ts, NSA_KV, NSA_REP, 3)
    gates = gates.transpose(0, 2, 3, 1, 4).reshape(bs, NSA_KV, -1, 3)
    o_cmp, sel = _cmp_sample(qg, kcvc, ts, past_len, nb, 8)
    n_sel = min(NSA_TOPK, nb)
    sel = sel[:, :, :n_sel].reshape(bs, NSA_KV * ts * n_sel)
    o_nsa, win_new = _slc_sample(page_table, sel, qg, gates, o_cmp, pool_t, f["kv_t"], win_t, f["win_t"],
                                 ts, past_len)
    o_nsa = o_nsa.reshape(bs, NSA_KV, NSA_REP, ts, NSA_HD).transpose(0, 3, 1, 2, 4).reshape(bs, ts, NSA_W)
    q1 = f["qlat"][0].reshape(MLA_HEADS, bs, ts, MLA_KVLORA).transpose(1, 2, 0, 3).reshape(bs, -1, MLA_KVLORA)
    q2 = jnp.pad(f["qrp"].reshape(bs, ts * MLA_HEADS, MLA_ROPE), ((0, 0), (0, 0), (0, LANE - MLA_ROPE)))
    o_mla = _mla_sample(page_table, q1, q2, mla_t, f["latx"], p["w_uv_cat"], ts, min(64, n_pages))
    pad_t = 16
    qmem = jnp.pad(f["qmem"].reshape(bs, ts, MEM_W), ((0, 0), (0, pad_t - ts), (0, 0)))
    o_mem = _mem_attend_cache(qmem, mem_kv, 8)[:, :ts]
    y = _back_layer(x, o_nsa, o_mla, o_mem, p, min(256, n_tok))
    return dict(f, kcvc=kcvc, o_cmp=o_cmp, sel=sel, o_nsa=o_nsa, win_new=win_new, o_mla=o_mla, o_mem=o_mem, y=y)


def kernel(x_prompt, x_sample, cache_nsa_kv, cache_mla, cache_win_kv, cache_mem_kv, page_table, mem_prompt,
           g_pre, w_in, nsa_w_cmp_k, nsa_w_cmp_v, mla_g_q, mla_w_uq, mla_g_kv, mla_w_uk, mla_w_uv,
           mem_g, mem_w_kv, w_br_nsa, w_br_mla, w_br_mem, w_out, g_post):
    params = _prepare_weights(g_pre, w_in, nsa_w_cmp_k, nsa_w_cmp_v, mla_g_q, mla_w_uq, mla_g_kv, mla_w_uk,
                              mla_w_uv, mem_g, mem_w_kv, w_br_nsa, w_br_mla, w_br_mem, w_out, g_post)
    depth = w_in.shape[0]
    bp, tp, _ = x_prompt.shape
    bs, ts, _ = x_sample.shape
    n_pool = cache_nsa_kv.shape[1]
    wb = cache_win_kv.shape[2]
    n_mem = mem_prompt.shape[1]
    wp = min(NSA_WINDOW, tp)
    xp, xs = x_prompt, x_sample
    outs = [[] for _ in range(7)]
    for l in range(depth):
        pool_t = jnp.transpose(cache_nsa_kv[l], (0, 2, 3, 4, 1)).reshape(n_pool, 4 * NSA_KV * NSA_HD, PAGE_SIZE)
        mla_t = jnp.transpose(cache_mla[l], (0, 2, 1))
        win_t = jnp.transpose(cache_win_kv[l], (0, 2, 3, 4, 1)).reshape(bs, 2 * NSA_KV * NSA_HD, wb)
        mem_kv_s = cache_mem_kv[l].reshape(bs, n_mem * 2 * MEM_HEADS, MEM_HD)
        rp = _prompt_layer(xp, mem_prompt, params, l)
        rs = _sample_layer(xs, pool_t, mla_t, win_t, mem_kv_s, page_table, params, l)
        xp, xs = rp["y"], rs["y"]
        outs[0].append(rp["rows_t"].reshape(bp, 4, NSA_KV, NSA_HD, tp).transpose(0, 4, 1, 2, 3))
        outs[1].append(rs["rows_t"][0].T.reshape(bs, ts, 4, NSA_KV, NSA_HD))
        outs[2].append(rp["lat_t"].transpose(0, 2, 1))
        outs[3].append(rs["lat_t"][0].T.reshape(bs, ts, LATENT))
        outs[4].append(rp["win_t"][:, :, tp - wp:].reshape(bp, 2, NSA_KV, NSA_HD, wp).transpose(0, 4, 1, 2, 3))
        outs[5].append(rs["win_new"].reshape(bs, 2, NSA_KV, NSA_HD, wb).transpose(0, 4, 1, 2, 3))
        outs[6].append(rp["mem_kv"].reshape(bp, n_mem, 2, MEM_HEADS, MEM_HD))
    return (xp, xs) + tuple(jnp.stack(o, axis=0) for o in outs)
```

```python
import functools

import numpy as np
import jax
import jax.numpy as jnp
from jax import lax
from jax.experimental import pallas as pl
from jax.experimental.pallas import tpu as pltpu

F32 = jnp.float32
BF16 = jnp.bfloat16

D_MODEL = 1024
PAGE_SIZE = 128
EPS = 1e-6
ROPE_THETA = 500000.0
NSA_HEADS = 8
NSA_KV = 2
NSA_REP = NSA_HEADS // NSA_KV
NSA_HD = 64
NSA_ROT = NSA_HD // 4
NSA_BLK = 64
NSA_TOPK = 16
NSA_WINDOW = 512
NSA_SCALE = NSA_HD ** -0.5
MLA_HEADS = 8
MLA_QLORA = 384
MLA_KVLORA = 256
MLA_NOPE = 64
MLA_ROPE = 32
MLA_V = 64
MLA_SCALE = (MLA_NOPE + MLA_ROPE) ** -0.5
LATENT = MLA_KVLORA + MLA_ROPE
MEM_HEADS = 4
MEM_HD = 128
MEM_SCALE = MEM_HD ** -0.5
NSA_W = NSA_HEADS * NSA_HD
MLA_W = MLA_HEADS * MLA_V
MEM_W = MEM_HEADS * MEM_HD
N_BRANCH = 3
NSA_KV_W = 3 * 2 * NSA_KV * NSA_HD
IN_SPLITS = (NSA_W, NSA_KV_W, NSA_HEADS * 3, NSA_W,
             MLA_QLORA, MLA_KVLORA, MLA_ROPE, MLA_W,
             MEM_W, MEM_W, N_BRANCH * D_MODEL)
NEG = -1e30
TINY = 1e-30
SEL_BONUS = 1e3
LANE = 128
ROPE_REP = LANE // MLA_ROPE
LATX = MLA_KVLORA + LANE
VMEM_LIMIT = 56 * 1024 * 1024


def _cparams(sem):
    return pltpu.CompilerParams(dimension_semantics=sem, vmem_limit_bytes=VMEM_LIMIT)


def _dot(a, b):
    return jnp.dot(a, b, preferred_element_type=F32)


def _dot_nt(a, b):
    return lax.dot_general(a, b, (((1,), (1,)), ((), ())), preferred_element_type=F32)


def _rms_scale(x):
    return lax.rsqrt(jnp.mean(x * x, axis=-1, keepdims=True) + EPS)


def _softmax_rows(s, mask):
    s = jnp.where(mask, s, NEG)
    p = jnp.where(mask, jnp.exp(s - jnp.max(s, axis=-1, keepdims=True)), 0.0)
    return p / jnp.maximum(jnp.sum(p, axis=-1, keepdims=True), TINY)


def _softmax_rows2(sa, mask_a, sb, mask_b):
    sa = jnp.where(mask_a, sa, NEG)
    sb = jnp.where(mask_b, sb, NEG)
    m = jnp.maximum(jnp.max(sa, axis=-1, keepdims=True), jnp.max(sb, axis=-1, keepdims=True))
    pa = jnp.where(mask_a, jnp.exp(sa - m), 0.0)
    pb = jnp.where(mask_b, jnp.exp(sb - m), 0.0)
    den = jnp.maximum(jnp.sum(pa, axis=-1, keepdims=True) + jnp.sum(pb, axis=-1, keepdims=True), TINY)
    return pa / den, pb / den


def _rope_lanes(x, cs_ref, group, half):
    w = x.shape[1]
    reps = w // LANE
    cos = jnp.concatenate([cs_ref[0]] * reps, axis=1)
    sin = jnp.concatenate([cs_ref[1]] * reps, axis=1)
    lane = lax.broadcasted_iota(jnp.int32, x.shape, 1)
    up = pltpu.roll(x, w - half, 1)
    down = pltpu.roll(x, half, 1)
    rot = jnp.where(lane % group < half, up, down)
    return x * cos + rot * sin


def _front_kernel(x_ref, gpre_ref, wr_ref, wt_ref, gq_ref, wuq_ref, gkv_ref, wuk_ref,
                  csn_ref, csm_ref, tn_ref, tm_ref,
                  q_ref, rc_ref, bg_ref, qmem_ref, qlat_ref, qrp_ref,
                  rows_t_ref, win_t_ref, kv_t_ref, lat_t_ref, latx_ref):
    x = x_ref[0]
    h = (x * _rms_scale(x) * gpre_ref[...]).astype(BF16)
    yr = _dot(h, wr_ref[...])
    q = _rope_lanes(yr[:, 0:NSA_W], csn_ref, NSA_HD, NSA_ROT // 2)
    q_ref[0] = (q * NSA_SCALE).astype(BF16)
    kc = _rope_lanes(yr[:, 512:640], csn_ref, NSA_HD, NSA_ROT // 2)
    rc_ref[0] = jnp.concatenate([kc, yr[:, 640:768]], axis=1).astype(BF16)
    cq = yr[:, 768:1152]
    qmem_ref[0] = yr[:, 1152:1664].astype(BF16)
    bg_ref[0] = jax.nn.sigmoid(yr[:, 1664:1792])
    cqn = (cq * _rms_scale(cq) * gq_ref[...]).astype(BF16)
    qm = _dot(cqn, wuq_ref[...])
    qrp_ref[0] = (_rope_lanes(qm[:, 512:768], csm_ref, MLA_ROPE, MLA_ROPE // 2) * MLA_SCALE).astype(BF16)
    for hd in range(MLA_HEADS):
        qn = qm[:, hd * MLA_NOPE:(hd + 1) * MLA_NOPE].astype(BF16)
        qlat_ref[0, hd] = (_dot(qn, wuk_ref[hd]) * MLA_SCALE).astype(BF16)
    yt = _dot_nt(wt_ref[...], h)
    cn, sn = tn_ref[0], tn_ref[1]
    half = NSA_ROT // 2
    pieces = []
    for base in range(0, NSA_KV_W, NSA_HD):
        is_key = (base // (NSA_KV * NSA_HD)) % 2 == 0
        if is_key:
            x1 = yt[base:base + half]
            x2 = yt[base + half:base + 2 * half]
            pieces += [x1 * cn - x2 * sn, x2 * cn + x1 * sn, yt[base + 2 * half:base + NSA_HD]]
        else:
            pieces.append(yt[base:base + NSA_HD])
    kvt = jnp.concatenate(pieces, axis=0)
    rows_t_ref[0] = kvt[0:512]
    win_t_ref[0] = kvt[512:768]
    kv_t_ref[0] = kvt[256:768].astype(BF16)
    ckv = yt[768:1024]
    ckvn = ckv * lax.rsqrt(jnp.mean(ckv * ckv, axis=0, keepdims=True) + EPS) * gkv_ref[...]
    cm, sm = tm_ref[0], tm_ref[1]
    k1 = yt[1024:1040]
    k2 = yt[1040:1056]
    kr = jnp.concatenate([k1 * cm - k2 * sm, k2 * cm + k1 * sm], axis=0)
    lat_t_ref[0] = jnp.concatenate([ckvn, kr], axis=0)
    krb = kr.astype(BF16)
    latx_ref[0] = jnp.concatenate([ckvn.astype(BF16)] + [krb] * ROPE_REP, axis=0)


def _rope_tables(pos):
    pos = pos.astype(F32)

    def tables(n_rot, group):
        half = n_rot // 2
        inv = ROPE_THETA ** (-jnp.arange(half, dtype=F32) / half)
        ang = pos[:, None] * inv[None, :]
        cos, sin = jnp.cos(ang), jnp.sin(ang)
        t = pos.shape[0]
        pad1 = jnp.ones((t, group - n_rot), F32)
        pad0 = jnp.zeros((t, group - n_rot), F32)
        cos_g = jnp.concatenate([cos, cos, pad1], axis=1)
        sin_g = jnp.concatenate([-sin, sin, pad0], axis=1)
        reps = LANE // group
        row = jnp.stack([jnp.tile(cos_g, (1, reps)), jnp.tile(sin_g, (1, reps))])
        col = jnp.stack([cos.T, sin.T])
        return row, col

    csn, tn = tables(NSA_ROT, NSA_HD)
    csm, tm = tables(MLA_ROPE, MLA_ROPE)
    return csn, csm, tn, tm


def _front_weights(w_in, mla_w_uq):
    o = np.cumsum((0,) + IN_SPLITS)
    w_q, w_kv, w_bg = w_in[:, o[0]:o[1]], w_in[:, o[1]:o[2]], w_in[:, o[2]:o[3]]
    w_cq, w_ckv, w_kr = w_in[:, o[4]:o[5]], w_in[:, o[5]:o[6]], w_in[:, o[6]:o[7]]
    w_qmem = w_in[:, o[8]:o[9]]
    pad = jnp.zeros((D_MODEL, LANE - NSA_HEADS * 3), F32)
    w_row = jnp.concatenate([w_q, w_kv[:, 0:256], w_cq, w_qmem, w_bg, pad], axis=1).astype(BF16)
    w_t = jnp.concatenate([w_kv, w_ckv, w_kr], axis=1).T.astype(BF16)
    wu = mla_w_uq.reshape(MLA_QLORA, MLA_HEADS, MLA_NOPE + MLA_ROPE)
    w_uq = jnp.concatenate([wu[:, :, :MLA_NOPE].reshape(MLA_QLORA, -1),
                            wu[:, :, MLA_NOPE:].reshape(MLA_QLORA, -1)], axis=1).astype(BF16)
    return w_row, w_t, w_uq


def _front(x, pos, g_pre, w_row, w_t, g_q, w_uq, g_kv, w_uk, tm):
    b, t, _ = x.shape
    csn, csm, tn, tmm = _rope_tables(pos)
    nt = t // tm
    tok = lambda w: pl.BlockSpec((1, tm, w), lambda i, j: (i, j, 0))
    tlay = lambda r: pl.BlockSpec((1, r, tm), lambda i, j: (i, 0, j))
    full = lambda a: pl.BlockSpec(a.shape, lambda i, j: (0,) * a.ndim)
    g_pre2, g_q2 = g_pre.reshape(1, -1), g_q.reshape(1, -1)
    g_kv2 = g_kv.reshape(-1, 1)
    w_uk = w_uk.astype(BF16)
    in_specs = [tok(D_MODEL), full(g_pre2), full(w_row), full(w_t), full(g_q2), full(w_uq), full(g_kv2),
                full(w_uk),
                pl.BlockSpec((2, tm, LANE), lambda i, j: (0, j, 0)),
                pl.BlockSpec((2, tm, LANE), lambda i, j: (0, j, 0)),
                pl.BlockSpec((2, NSA_ROT // 2, tm), lambda i, j: (0, 0, j)),
                pl.BlockSpec((2, MLA_ROPE // 2, tm), lambda i, j: (0, 0, j))]
    out_shape = [
        jax.ShapeDtypeStruct((b, t, NSA_W), BF16),
        jax.ShapeDtypeStruct((b, t, 256), BF16),
        jax.ShapeDtypeStruct((b, t, LANE), F32),
        jax.ShapeDtypeStruct((b, t, MEM_W), BF16),
        jax.ShapeDtypeStruct((b, MLA_HEADS, t, MLA_KVLORA), BF16),
        jax.ShapeDtypeStruct((b, t, MLA_HEADS * MLA_ROPE), BF16),
        jax.ShapeDtypeStruct((b, 512, t), F32),
        jax.ShapeDtypeStruct((b, 256, t), F32),
        jax.ShapeDtypeStruct((b, 512, t), BF16),
        jax.ShapeDtypeStruct((b, LATENT, t), F32),
        jax.ShapeDtypeStruct((b, LATX, t), BF16),
    ]
    out_specs = [tok(NSA_W), tok(256), tok(LANE), tok(MEM_W),
                 pl.BlockSpec((1, MLA_HEADS, tm, MLA_KVLORA), lambda i, j: (i, 0, j, 0)),
                 tok(MLA_HEADS * MLA_ROPE),
                 tlay(512), tlay(256), tlay(512), tlay(LATENT), tlay(LATX)]
    return pl.pallas_call(
        _front_kernel, grid=(b, nt), in_specs=in_specs, out_specs=out_specs, out_shape=out_shape,
        compiler_params=_cparams(("parallel", "parallel")), name="front",
    )(x, g_pre2, w_row, w_t, g_q2, w_uq, g_kv2, w_uk, csn, csm, tn, tmm)


def _compress_weights(wk, wv):
    z = jnp.zeros_like(wk)
    rows = [jnp.concatenate(r, axis=2) for r in ([wk, z, z, z], [z, wk, z, z], [z, z, wv, z], [z, z, z, wv])]
    return jnp.concatenate(rows, axis=1).astype(BF16)


def _compress_kernel(x_ref, w_ref, o_ref):
    @pl.when(pl.program_id(0) == 0)
    def _():
        o_ref[...] = jnp.zeros_like(o_ref)

    o_ref[...] += _dot(x_ref[...], w_ref[0])


def _compress(x, w_bd, n_l):
    r = x.shape[0]
    return pl.pallas_call(
        _compress_kernel, grid=(n_l,),
        in_specs=[pl.BlockSpec((r, 256), lambda l: (0, l)),
                  pl.BlockSpec((1, 256, 256), lambda l: (l, 0, 0))],
        out_specs=pl.BlockSpec((r, 256), lambda l: (0, 0)),
        out_shape=jax.ShapeDtypeStruct((r, 256), F32),
        compiler_params=_cparams(("arbitrary",)), name="nsa_compress",
    )(x, w_bd)


def _select_blocks(score, nb, n_sel):
    r = score.shape[0]
    st = score.T[0:nb]
    blk = lax.broadcasted_iota(jnp.int32, (nb, 1), 0)
    rank = jnp.zeros((nb, r), F32)
    for m in range(nb):
        sm = st[m:m + 1]
        rank = rank + jnp.where((sm > st) | ((sm == st) & (m < blk)), 1.0, 0.0)
    sel_t = jnp.where(rank < n_sel, 1.0, 0.0)
    return jnp.concatenate([sel_t, jnp.zeros((LANE - nb, r), F32)], axis=0).T


ATT_KC = 512


def _lane_tiles(x):
    return [x[:, j * LANE:(j + 1) * LANE] for j in range(x.shape[1] // LANE)]


def _tile_max(x):
    return functools.reduce(jnp.maximum, _lane_tiles(x))


def _nsa_prompt_kernel(q_ref, bg_ref, kcvc_ref, kv_ref, o_ref, bias_ref, s_buf, mx_ref, l_ref, acc_ref,
                       *, tq, t_len):
    nb = t_len // NSA_BLK
    kc = min(ATT_KC, t_len)
    q0 = pl.program_id(1) * tq
    n_kc = (q0 + tq + kc - 1) // kc
    qpos = q0 + lax.broadcasted_iota(jnp.int32, (tq, 1), 0)
    blk = lax.broadcasted_iota(jnp.int32, (1, LANE), 1)
    vis = ((blk + 1) * NSA_BLK <= qpos + 1) & (blk < nb)
    cur = qpos // NSA_BLK
    valid = (blk <= cur) & (blk < nb)
    forced = (blk == 0) | (blk == cur) | (blk == cur - 1)
    key = lax.broadcasted_iota(jnp.int32, (1, t_len), 1)
    expand = (lax.broadcasted_iota(jnp.int32, (LANE, t_len), 1) // NSA_BLK
              == lax.broadcasted_iota(jnp.int32, (LANE, t_len), 0)).astype(BF16)
    wlen = min(NSA_WINDOW + tq, t_len)
    wstart = pl.multiple_of(jnp.maximum(q0 - NSA_WINDOW, 0), LANE)
    wdist = qpos - (wstart + lax.broadcasted_iota(jnp.int32, (1, wlen), 1))
    wbias = jnp.where((wdist >= 0) & (wdist < NSA_WINDOW), 0.0, NEG)
    q = q_ref[0]
    gates = bg_ref[0]
    kcvc = kcvc_ref[0].astype(BF16)
    qs = [q[:, h * NSA_HD:(h + 1) * NSA_HD] for h in range(NSA_HEADS)]
    o_cmp = []
    for g in range(NSA_KV):
        kcg = kcvc[:, g * NSA_HD:(g + 1) * NSA_HD]
        vcg = kcvc[:, 128 + g * NSA_HD:128 + (g + 1) * NSA_HD]
        psum = jnp.zeros((tq, LANE), F32)
        for r in range(NSA_REP):
            p = _softmax_rows(_dot_nt(qs[g * NSA_REP + r], kcg), vis)
            o_cmp.append(_dot(p.astype(BF16), vcg))
            psum = psum + p
        score = jnp.where(valid, psum + jnp.where(forced, SEL_BONUS, 0.0), -jnp.inf)
        sel = _select_blocks(score, nb, min(NSA_TOPK, nb))
        bias_ref[g] = jnp.where((_dot(sel.astype(BF16), expand) > 0.5) & (key <= qpos), 0.0, NEG)
    mx_ref[...] = jnp.full(mx_ref.shape, NEG, F32)
    l_ref[...] = jnp.zeros(l_ref.shape, F32)
    acc_ref[...] = jnp.zeros(acc_ref.shape, F32)

    def pass1(c, carry):
        k0 = pl.multiple_of(c * kc, kc)
        for h in range(NSA_HEADS):
            g = h // NSA_REP
            s = _dot(qs[h], kv_ref[0, g * NSA_HD:(g + 1) * NSA_HD, pl.ds(k0, kc)]) + bias_ref[g, :, pl.ds(k0, kc)]
            s_buf[h, :, pl.ds(k0, kc)] = s
            mx_ref[h] = jnp.maximum(mx_ref[h], _tile_max(s))
        return carry

    lax.fori_loop(0, n_kc, pass1, 0)
    for h in range(NSA_HEADS):
        mx_ref[h] = jnp.broadcast_to(jnp.max(mx_ref[h], axis=1, keepdims=True), (tq, LANE))

    def pass2(c, carry):
        k0 = pl.multiple_of(c * kc, kc)
        for h in range(NSA_HEADS):
            g = h // NSA_REP
            m = mx_ref[h]
            ps = [jnp.exp(t - m) for t in _lane_tiles(s_buf[h, :, pl.ds(k0, kc)])]
            l_ref[h] += functools.reduce(jnp.add, ps)
            p = jnp.concatenate(ps, axis=1).astype(BF16)
            acc_ref[h] += _dot_nt(p, kv_ref[0, 128 + g * NSA_HD:128 + (g + 1) * NSA_HD, pl.ds(k0, kc)])
        return carry

    lax.fori_loop(0, n_kc, pass2, 0)
    outs = []
    for h in range(NSA_HEADS):
        g = h // NSA_REP
        o_slc = acc_ref[h] / jnp.sum(l_ref[h], axis=1, keepdims=True)
        kw = kv_ref[0, 256 + g * NSA_HD:256 + (g + 1) * NSA_HD, pl.ds(wstart, wlen)]
        vw = kv_ref[0, 384 + g * NSA_HD:384 + (g + 1) * NSA_HD, pl.ds(wstart, wlen)]
        s = _dot(qs[h], kw) + wbias
        p = jnp.exp(s - jnp.max(s, axis=1, keepdims=True))
        o_win = _dot_nt(p.astype(BF16), vw) / jnp.sum(p, axis=1, keepdims=True)
        outs.append(gates[:, 3 * h:3 * h + 1] * o_cmp[h] + gates[:, 3 * h + 1:3 * h + 2] * o_slc
                    + gates[:, 3 * h + 2:3 * h + 3] * o_win)
    o_ref[0] = jnp.concatenate(outs, axis=1)


def _nsa_prompt(q, bg, kcvc, kv_t, tq):
    b, t, _ = q.shape
    tok = lambda w: pl.BlockSpec((1, tq, w), lambda i, j: (i, j, 0))
    return pl.pallas_call(
        functools.partial(_nsa_prompt_kernel, tq=tq, t_len=t), grid=(b, t // tq),
        in_specs=[tok(NSA_W), tok(LANE),
                  pl.BlockSpec((1, LANE, 256), lambda i, j: (i, 0, 0)),
                  pl.BlockSpec((1, 512, t), lambda i, j: (i, 0, 0))],
        out_specs=tok(NSA_W),
        out_shape=jax.ShapeDtypeStruct((b, t, NSA_W), F32),
        scratch_shapes=[pltpu.VMEM((NSA_KV, tq, t), F32), pltpu.VMEM((NSA_HEADS, tq, t), F32),
                        pltpu.VMEM((NSA_HEADS, tq, LANE), F32), pltpu.VMEM((NSA_HEADS, tq, LANE), F32),
                        pltpu.VMEM((NSA_HEADS, tq, NSA_HD), F32)],
        compiler_params=_cparams(("parallel", "parallel")), name="nsa_prompt",
    )(q, bg, kcvc, kv_t)


def _stack_rope_queries(qrp):
    r = qrp.shape[0]
    slot = lax.broadcasted_iota(jnp.int32, (r, LANE), 1) // MLA_ROPE
    parts = []
    for h in range(MLA_HEADS):
        tile = qrp[:, (h // ROPE_REP) * LANE:(h // ROPE_REP + 1) * LANE]
        parts.append(jnp.where(slot == h % ROPE_REP, tile, jnp.zeros_like(tile)))
    return jnp.concatenate(parts, axis=0)


def _mla_prompt_kernel(qlat_ref, qrp_ref, latx_ref, wuv_ref, o_ref, s_buf, mx_ref, l_ref, acc_ref, *, tq, t_len):
    kc = min(ATT_KC, t_len)
    rows = MLA_HEADS * tq
    q0 = pl.program_id(1) * tq
    n_full = q0 // kc
    q1 = qlat_ref[0].reshape(rows, MLA_KVLORA)
    q2 = _stack_rope_queries(qrp_ref[0])

    def scores(c):
        k0 = pl.multiple_of(c * kc, kc)
        return (_dot(q1, latx_ref[0, 0:MLA_KVLORA, pl.ds(k0, kc)])
                + _dot(q2, latx_ref[0, MLA_KVLORA:, pl.ds(k0, kc)]))

    mx_ref[...] = jnp.full(mx_ref.shape, NEG, F32)
    l_ref[...] = jnp.zeros(l_ref.shape, F32)
    acc_ref[...] = jnp.zeros(acc_ref.shape, F32)

    def pass1(c, carry):
        s = scores(c)
        s_buf[:, pl.ds(pl.multiple_of(c * kc, kc), kc)] = s
        mx_ref[...] = jnp.maximum(mx_ref[...], _tile_max(s))
        return carry

    lax.fori_loop(0, n_full, pass1, 0)
    qpos = q0 + lax.broadcasted_iota(jnp.int32, (rows, 1), 0) % tq
    key = n_full * kc + lax.broadcasted_iota(jnp.int32, (1, kc), 1)
    s = scores(n_full) + jnp.where(key <= qpos, 0.0, NEG)
    s_buf[:, pl.ds(pl.multiple_of(n_full * kc, kc), kc)] = s
    m = jnp.max(jnp.maximum(mx_ref[...], _tile_max(s)), axis=1, keepdims=True)
    mx_ref[...] = jnp.broadcast_to(m, (rows, LANE))

    def pass2(c, carry):
        k0 = pl.multiple_of(c * kc, kc)
        mm = mx_ref[...]
        ps = [jnp.exp(t - mm) for t in _lane_tiles(s_buf[:, pl.ds(k0, kc)])]
        l_ref[...] += functools.reduce(jnp.add, ps)
        p = jnp.concatenate(ps, axis=1).astype(BF16)
        acc_ref[...] += _dot_nt(p, latx_ref[0, 0:MLA_KVLORA, pl.ds(k0, kc)])
        return carry

    lax.fori_loop(0, n_full + 1, pass2, 0)
    o_lat = (acc_ref[...] / jnp.sum(l_ref[...], axis=1, keepdims=True)).astype(BF16)
    o_ref[0] = jnp.concatenate([_dot(o_lat[h * tq:(h + 1) * tq], wuv_ref[h]) for h in range(MLA_HEADS)], axis=1)


def _mla_prompt(qlat, qrp, latx, w_uv, tq):
    b, _, t, _ = qlat.shape
    return pl.pallas_call(
        functools.partial(_mla_prompt_kernel, tq=tq, t_len=t), grid=(b, t // tq),
        in_specs=[pl.BlockSpec((1, MLA_HEADS, tq, MLA_KVLORA), lambda i, j: (i, 0, j, 0)),
                  pl.BlockSpec((1, tq, 256), lambda i, j: (i, j, 0)),
                  pl.BlockSpec((1, LATX, t), lambda i, j: (i, 0, 0)),
                  pl.BlockSpec(w_uv.shape, lambda i, j: (0, 0, 0))],
        out_specs=pl.BlockSpec((1, tq, MLA_W), lambda i, j: (i, j, 0)),
        out_shape=jax.ShapeDtypeStruct((b, t, MLA_W), F32),
        scratch_shapes=[pltpu.VMEM((MLA_HEADS * tq, t), F32), pltpu.VMEM((MLA_HEADS * tq, LANE), F32),
                        pltpu.VMEM((MLA_HEADS * tq, LANE), F32), pltpu.VMEM((MLA_HEADS * tq, MLA_KVLORA), F32)],
        compiler_params=_cparams(("parallel", "parallel")), name="mla_prompt",
    )(qlat, qrp, latx, w_uv)


def _mem_kv_kernel(m_ref, g_ref, w_ref, o_ref):
    m = m_ref[...]
    o_ref[...] = _dot((m * _rms_scale(m) * g_ref[...]).astype(BF16), w_ref[...])


def _mem_kv(mem, g, w, tm):
    n = mem.shape[0]
    g2 = g.reshape(1, -1)
    return pl.pallas_call(
        _mem_kv_kernel, grid=(n // tm,),
        in_specs=[pl.BlockSpec((tm, D_MODEL), lambda i: (i, 0)),
                  pl.BlockSpec(g2.shape, lambda i: (0, 0)),
                  pl.BlockSpec(w.shape, lambda i: (0, 0))],
        out_specs=pl.BlockSpec((tm, 2 * MEM_W), lambda i: (i, 0)),
        out_shape=jax.ShapeDtypeStruct((n, 2 * MEM_W), F32),
        compiler_params=_cparams(("parallel",)), name="mem_kv",
    )(mem, g2, w)


def _mem_attend_kernel(q_ref, kv_ref, o_ref):
    q = q_ref[0]
    outs = []
    for h in range(MEM_HEADS):
        k = kv_ref[0, :, h * MEM_HD:(h + 1) * MEM_HD].astype(BF16)
        v = kv_ref[0, :, MEM_W + h * MEM_HD:MEM_W + (h + 1) * MEM_HD].astype(BF16)
        s = _dot_nt(q[:, h * MEM_HD:(h + 1) * MEM_HD], k) * MEM_SCALE
        e = jnp.exp(s - jnp.max(s, axis=-1, keepdims=True))
        p = e / jnp.sum(e, axis=-1, keepdims=True)
        outs.append(_dot(p.astype(BF16), v))
    o_ref[0] = jnp.concatenate(outs, axis=1)


def _mem_attend(q, kv, tq):
    b, t, _ = q.shape
    return pl.pallas_call(
        _mem_attend_kernel, grid=(b, t // tq),
        in_specs=[pl.BlockSpec((1, tq, MEM_W), lambda i, j: (i, j, 0)),
                  pl.BlockSpec((1,) + kv.shape[1:], lambda i, j: (i, 0, 0))],
        out_specs=pl.BlockSpec((1, tq, MEM_W), lambda i, j: (i, j, 0)),
        out_shape=jax.ShapeDtypeStruct((b, t, MEM_W), F32),
        compiler_params=_cparams(("parallel", "parallel")), name="mem_attend",
    )(q, kv)


def _mem_attend_cache_kernel(q_ref, kv_ref, o_ref):
    n_mem = kv_ref.shape[1] // (2 * MEM_HEADS)
    for bb in range(q_ref.shape[0]):
        q = q_ref[bb]
        outs = []
        for h in range(MEM_HEADS):
            k = kv_ref[bb, pl.ds(h, n_mem, stride=2 * MEM_HEADS), :].astype(BF16)
            v = kv_ref[bb, pl.ds(MEM_HEADS + h, n_mem, stride=2 * MEM_HEADS), :].astype(BF16)
            s = _dot_nt(q[:, h * MEM_HD:(h + 1) * MEM_HD], k) * MEM_SCALE
            e = jnp.exp(s - jnp.max(s, axis=-1, keepdims=True))
            p = e / jnp.sum(e, axis=-1, keepdims=True)
            outs.append(_dot(p.astype(BF16), v))
        o_ref[bb] = jnp.concatenate(outs, axis=1)


def _mem_attend_cache(q, kv, n_b):
    b, t, _ = q.shape
    return pl.pallas_call(
        _mem_attend_cache_kernel, grid=(b // n_b,),
        in_specs=[pl.BlockSpec((n_b, t, MEM_W), lambda i: (i, 0, 0)),
                  pl.BlockSpec((n_b,) + kv.shape[1:], lambda i: (i, 0, 0))],
        out_specs=pl.BlockSpec((n_b, t, MEM_W), lambda i: (i, 0, 0)),
        out_shape=jax.ShapeDtypeStruct((b, t, MEM_W), F32),
        compiler_params=_cparams(("parallel",)), name="mem_attend_cache",
    )(q, kv)


def _back_kernel(x_ref, on_ref, om_ref, oe_ref, gpre_ref, wz_ref, wg_ref, wbr_ref, wout_ref, gpost_ref, y_ref):
    x = x_ref[...]
    h = (x * _rms_scale(x) * gpre_ref[...]).astype(BF16)
    y = jnp.zeros(x.shape, F32)
    for j, o_ref in enumerate((on_ref, om_ref, oe_ref)):
        z = _dot(h, wz_ref[j])
        u = (o_ref[...] * (z * jax.nn.sigmoid(z))).astype(BF16)
        gate = jax.nn.sigmoid(_dot(h, wg_ref[j]))
        y = y + gate * _dot(u, wbr_ref[j])
    v = _dot(y.astype(BF16), wout_ref[...])
    y_ref[...] = x + v * _rms_scale(v) * gpost_ref[...]


def _back_weights(w_in, w_br_nsa, w_br_mla, w_br_mem):
    o = np.cumsum((0,) + IN_SPLITS)
    w_z = jnp.stack([w_in[:, o[3]:o[4]], w_in[:, o[7]:o[8]], w_in[:, o[9]:o[10]]]).astype(BF16)
    w_g = jnp.stack([w_in[:, o[10] + j * D_MODEL:o[10] + (j + 1) * D_MODEL] for j in range(N_BRANCH)]).astype(BF16)
    w_br = jnp.stack([w_br_nsa, w_br_mla, w_br_mem]).astype(BF16)
    return w_z, w_g, w_br


def _back(x, o_nsa, o_mla, o_mem, g_pre, w_z, w_g, w_br, w_out, g_post, tm):
    n = x.shape[0]
    g_pre2, g_post2 = g_pre.reshape(1, -1), g_post.reshape(1, -1)
    full = lambda a: pl.BlockSpec(a.shape, lambda i: (0,) * a.ndim)
    tok = lambda w: pl.BlockSpec((tm, w), lambda i: (i, 0))
    return pl.pallas_call(
        _back_kernel, grid=(n // tm,),
        in_specs=[tok(D_MODEL), tok(512), tok(512), tok(512), full(g_pre2), full(w_z), full(w_g), full(w_br),
                  full(w_out), full(g_post2)],
        out_specs=tok(D_MODEL),
        out_shape=jax.ShapeDtypeStruct((n, D_MODEL), F32),
        compiler_params=_cparams(("parallel",)), name="back",
    )(x, o_nsa, o_mla, o_mem, g_pre2, w_z, w_g, w_br, w_out, g_post2)


def _div(a, b):
    return lax.div(a, jnp.int32(b))


def _rem(a, b):
    return lax.rem(a, jnp.int32(b))


def _pipelined_gather(step, n_steps, n_copies, make_copies):
    slot = _rem(step, 2)

    def run(s, sl, start):
        def body(i, carry):
            for cp in make_copies(s, i, sl):
                if start:
                    cp.start()
                else:
                    cp.wait()
            return carry
        lax.fori_loop(0, n_copies, body, 0)

    @pl.when(step == 0)
    def _():
        run(step, slot, True)

    @pl.when(step + 1 < n_steps)
    def _():
        run(step + 1, 1 - slot, True)

    run(step, slot, False)
    return slot


def _past_compress_weights(wk, wv):
    def one(w):
        return jnp.einsum('jk,lde->djlke', jnp.eye(2, dtype=F32), w).reshape(NSA_HD * PAGE_SIZE, 128)
    return jnp.stack([one(wk), one(wv)]).astype(BF16)


def _past_compress_kernel(pt_ref, pool_ref, w_ref, o_ref, buf, a_buf, sem, *, n_pages):
    chunk = buf.shape[2]
    steps_per_b = n_pages // chunk
    step = pl.program_id(0) * steps_per_b + pl.program_id(1)

    def page_copy(s, i, slot):
        page = pt_ref[_div(s, steps_per_b), _rem(s, steps_per_b) * chunk + i]
        return (pltpu.make_async_copy(pool_ref.at[page, pl.ds(0, 256), :], buf.at[slot, :, i, :], sem.at[slot]),)

    slot = _pipelined_gather(step, pl.num_programs(0) * steps_per_b, chunk, page_copy)

    for kv in range(2):
        for g in range(NSA_KV):
            for d in range(NSA_HD):
                a_buf[g * chunk:(g + 1) * chunk, d * PAGE_SIZE:(d + 1) * PAGE_SIZE] = (
                    buf[slot, kv * 128 + g * NSA_HD + d].astype(BF16))
        acc = _dot(a_buf[...], w_ref[kv])
        for g in range(NSA_KV):
            o_ref[0, kv, g] = acc[g * chunk:(g + 1) * chunk]


def _past_compress(page_table, pool_t, w2, chunk):
    bs, n_pages = page_table.shape
    grid_spec = pltpu.PrefetchScalarGridSpec(
        num_scalar_prefetch=1, grid=(bs, n_pages // chunk),
        in_specs=[pl.BlockSpec(memory_space=pl.ANY),
                  pl.BlockSpec(w2.shape, lambda b, c, pt: (0, 0, 0))],
        out_specs=pl.BlockSpec((1, 2, NSA_KV, chunk, 128), lambda b, c, pt: (b, 0, 0, c, 0)),
        scratch_shapes=[pltpu.VMEM((2, 256, chunk, PAGE_SIZE), F32),
                        pltpu.VMEM((NSA_KV * chunk, NSA_HD * PAGE_SIZE), BF16),
                        pltpu.SemaphoreType.DMA((2,))])
    out = pl.pallas_call(
        functools.partial(_past_compress_kernel, n_pages=n_pages), grid_spec=grid_spec,
        out_shape=jax.ShapeDtypeStruct((bs, 2, NSA_KV, n_pages, 128), F32),
        compiler_params=_cparams(("arbitrary", "arbitrary")), name="past_compress",
    )(page_table, pool_t, w2)
    return out.reshape(bs, 2, NSA_KV, n_pages * (PAGE_SIZE // NSA_BLK), NSA_HD)


def _cmp_sample_kernel(q_ref, kcvc_ref, oc_ref, sel_ref, sc_ref, *, ts, past_len, nb):
    nbp = kcvc_ref.shape[3]
    n_b = q_ref.shape[0]
    rows = NSA_REP * ts
    t_row = lax.broadcasted_iota(jnp.int32, (rows, 1), 0) % ts
    blk = lax.broadcasted_iota(jnp.int32, (1, nbp), 1)
    vis = ((blk + 1) * NSA_BLK <= past_len + t_row + 1) & (blk < nb)
    for bb in range(n_b):
        for g in range(NSA_KV):
            kc = kcvc_ref[bb, 0, g].astype(BF16)
            vc = kcvc_ref[bb, 1, g].astype(BF16)
            p = _softmax_rows(_dot_nt(q_ref[bb, g], kc), vis)
            oc_ref[bb, g] = _dot(p.astype(BF16), vc)
            psum = p[0:ts]
            for r in range(1, NSA_REP):
                psum = psum + p[r * ts:(r + 1) * ts]
            sc_ref[(bb * NSA_KV + g) * ts:(bb * NSA_KV + g + 1) * ts, :] = psum
    n_rows = n_b * NSA_KV * ts
    t_sel = lax.broadcasted_iota(jnp.int32, (n_rows, 1), 0) % ts
    cur = (past_len + t_sel) // NSA_BLK
    valid = (blk <= cur) & (blk < nb)
    forced = (blk == 0) | (blk == cur) | (blk == cur - 1)
    score = jnp.where(valid, sc_ref[...] + jnp.where(forced, SEL_BONUS, 0.0), -jnp.inf)
    blk_f = blk.astype(F32)
    taken = jnp.broadcast_to(blk >= nb, score.shape)
    col = lax.broadcasted_iota(jnp.int32, (n_rows, LANE), 1)
    sel = jnp.zeros((n_rows, LANE), F32)
    for k in range(min(NSA_TOPK, nb)):
        best = jnp.max(jnp.where(taken, -jnp.inf, score), axis=1, keepdims=True)
        idx = jnp.min(jnp.where((~taken) & (score == best), blk_f, float(nbp)), axis=1, keepdims=True)
        sel = jnp.where(col == k, idx, sel)
        taken = taken | (blk_f == idx)
    sel_ref[...] = sel.astype(jnp.int32).reshape(sel_ref.shape)


def _cmp_sample(qg, kcvc, ts, past_len, nb, n_b):
    bs = qg.shape[0]
    rows = NSA_REP * ts
    nbp = kcvc.shape[3]
    return pl.pallas_call(
        functools.partial(_cmp_sample_kernel, ts=ts, past_len=past_len, nb=nb), grid=(bs // n_b,),
        in_specs=[pl.BlockSpec((n_b, NSA_KV, rows, NSA_HD), lambda b: (b, 0, 0, 0)),
                  pl.BlockSpec((n_b, 2, NSA_KV, nbp, NSA_HD), lambda b: (b, 0, 0, 0, 0))],
        out_specs=[pl.BlockSpec((n_b, NSA_KV, rows, NSA_HD), lambda b: (b, 0, 0, 0)),
                   pl.BlockSpec((n_b, NSA_KV * ts, LANE), lambda b: (b, 0, 0))],
        out_shape=[jax.ShapeDtypeStruct((bs, NSA_KV, rows, NSA_HD), F32),
                   jax.ShapeDtypeStruct((bs, NSA_KV * ts, LANE), jnp.int32)],
        scratch_shapes=[pltpu.VMEM((n_b * NSA_KV * ts, nbp), F32)],
        compiler_params=_cparams(("parallel",)), name="cmp_sample",
    )(qg, kcvc)


def _slc_sample_kernel(pt_ref, sel_ref, q_ref, gate_ref, oc_ref, pool_ref, kvn_ref, win_ref, winn_ref,
                       o_ref, wout_ref, kvbuf, sem, *, ts, past_len, n_sel):
    b = pl.program_id(0)
    nbs = pl.num_programs(0)
    bpp = PAGE_SIZE // NSA_BLK
    nbp = pt_ref.shape[1] * bpp
    n_tg = NSA_KV * ts
    rows = NSA_REP * ts
    per_b = LANE // ts

    def copies(bb, gt, slot):
        krow = pl.multiple_of(256 + _div(gt, ts) * NSA_HD, NSA_HD)
        vrow = pl.multiple_of(krow + NSA_KV * NSA_HD, NSA_HD)
        out = []
        for k in range(n_sel):
            s = jnp.minimum(sel_ref[bb, gt * n_sel + k], nbp - 1)
            page = pt_ref[bb, _div(s, bpp)]
            dst = pl.ds(k * PAGE_SIZE, PAGE_SIZE)
            out.append(pltpu.make_async_copy(pool_ref.at[page, pl.ds(krow, NSA_HD), :],
                                             kvbuf.at[slot, gt, 0:NSA_HD, dst], sem.at[slot]))
            out.append(pltpu.make_async_copy(pool_ref.at[page, pl.ds(vrow, NSA_HD), :],
                                             kvbuf.at[slot, gt, NSA_HD:, dst], sem.at[slot]))
        return out

    slot = _pipelined_gather(b, nbs, n_tg, copies)

    t_row = lax.broadcasted_iota(jnp.int32, (rows, 1), 0) % ts
    lane_p = lax.broadcasted_iota(jnp.int32, (1, n_sel * PAGE_SIZE), 1)
    lane_n = lax.broadcasted_iota(jnp.int32, (1, LANE), 1)
    mine = lane_n // ts == b % per_b
    t_key = lane_n % ts
    wb = win_ref.shape[2]
    lane_w = lax.broadcasted_iota(jnp.int32, (1, wb), 1)
    wdist = t_row + wb - lane_w
    wmask = (wdist >= 0) & (wdist < NSA_WINDOW)
    wdist_n = t_row - t_key
    wmask_n = mine & (wdist_n >= 0) & (wdist_n < NSA_WINDOW)
    for g in range(NSA_KV):
        q = q_ref[0, g]
        gates = gate_ref[0, g]
        kn = kvn_ref[0, g * NSA_HD:(g + 1) * NSA_HD, :]
        vn = kvn_ref[0, 128 + g * NSA_HD:128 + (g + 1) * NSA_HD, :]
        s_new = _dot(q, kn)
        o_slc = jnp.zeros((rows, NSA_HD), F32)
        for t in range(ts):
            selv = jnp.zeros((1, n_sel * PAGE_SIZE), jnp.int32)
            chosen = jnp.zeros((1, LANE), jnp.bool_)
            for k in range(n_sel):
                s_k = sel_ref[b, (g * ts + t) * n_sel + k]
                selv = jnp.where(lane_p // PAGE_SIZE == k, s_k, selv)
                chosen = chosen | ((past_len + t_key) // NSA_BLK == s_k)
            pmask = (selv < nbp) & ((lane_p % PAGE_SIZE) // NSA_BLK == selv % bpp) & (t_row == t)
            nmask = mine & chosen & (t_key <= t_row) & (t_row == t)
            kt = kvbuf[slot, g * ts + t, 0:NSA_HD].astype(BF16)
            vt = kvbuf[slot, g * ts + t, NSA_HD:].astype(BF16)
            pa, pb = _softmax_rows2(_dot(q, kt), pmask, s_new, nmask)
            o_slc = o_slc + _dot_nt(pa.astype(BF16), vt) + _dot_nt(pb.astype(BF16), vn)
        kw = win_ref[0, g * NSA_HD:(g + 1) * NSA_HD, :].astype(BF16)
        vw = win_ref[0, 128 + g * NSA_HD:128 + (g + 1) * NSA_HD, :].astype(BF16)
        kwn = kvn_ref[0, 256 + g * NSA_HD:256 + (g + 1) * NSA_HD, :]
        vwn = kvn_ref[0, 384 + g * NSA_HD:384 + (g + 1) * NSA_HD, :]
        pa, pb = _softmax_rows2(_dot(q, kw), wmask, _dot(q, kwn), wmask_n)
        o_win = _dot_nt(pa.astype(BF16), vw) + _dot_nt(pb.astype(BF16), vwn)
        o_ref[0, g] = gates[:, 0:1] * oc_ref[0, g] + gates[:, 1:2] * o_slc + gates[:, 2:3] * o_win
    shifted = pltpu.roll(win_ref[0], wb - ts, 1)
    wout_ref[0] = shifted
    new_cols = pltpu.roll(winn_ref[0], (LANE - ts - (b % per_b) * ts) % LANE, 1)
    wout_ref[0, :, wb - LANE:] = jnp.where(lane_n >= LANE - ts, new_cols, shifted[:, wb - LANE:])


def _slc_sample(page_table, sel, qg, gates, o_cmp, pool_t, kvn_t, win_t, winn_t, ts, past_len):
    bs = qg.shape[0]
    rows = NSA_REP * ts
    n_sel = sel.shape[1] // (NSA_KV * ts)
    wb = win_t.shape[2]
    per_b = LANE // ts
    qspec = lambda w: pl.BlockSpec((1, NSA_KV, rows, w), lambda b, pt, sl: (b, 0, 0, 0))
    grid_spec = pltpu.PrefetchScalarGridSpec(
        num_scalar_prefetch=2, grid=(bs,),
        in_specs=[qspec(NSA_HD), qspec(3), qspec(NSA_HD),
                  pl.BlockSpec(memory_space=pl.ANY),
                  pl.BlockSpec((1, 512, LANE), lambda b, pt, sl: (0, 0, b // per_b)),
                  pl.BlockSpec((1, 256, wb), lambda b, pt, sl: (b, 0, 0)),
                  pl.BlockSpec((1, 256, LANE), lambda b, pt, sl: (0, 0, b // per_b))],
        out_specs=[qspec(NSA_HD), pl.BlockSpec((1, 256, wb), lambda b, pt, sl: (b, 0, 0))],
        scratch_shapes=[pltpu.VMEM((2, NSA_KV * ts, 2 * NSA_HD, n_sel * PAGE_SIZE), F32),
                        pltpu.SemaphoreType.DMA((2,))])
    return pl.pallas_call(
        functools.partial(_slc_sample_kernel, ts=ts, past_len=past_len, n_sel=n_sel), grid_spec=grid_spec,
        out_shape=[jax.ShapeDtypeStruct((bs, NSA_KV, rows, NSA_HD), F32),
                   jax.ShapeDtypeStruct((bs, 256, wb), F32)],
        compiler_params=_cparams(("arbitrary",)), name="slc_sample",
    )(page_table, sel, qg, gates, o_cmp, pool_t, kvn_t, win_t, winn_t)


def _mla_sample_kernel(pt_ref, q1_ref, q2_ref, pool_ref, latn_ref, wuv_ref, o_ref,
                       buf, kb, m_ref, l_ref, acc_ref, sem, *, ts, n_pages):
    n_chunk = buf.shape[2] // PAGE_SIZE
    steps_per_b = n_pages // n_chunk
    b, c = pl.program_id(0), pl.program_id(1)
    step = b * steps_per_b + c
    rows = MLA_HEADS * ts

    def page_copy(s, i, slot):
        page = pt_ref[_div(s, steps_per_b), _rem(s, steps_per_b) * n_chunk + i]
        dst = buf.at[slot, :, pl.ds(pl.multiple_of(i * PAGE_SIZE, PAGE_SIZE), PAGE_SIZE)]
        return (pltpu.make_async_copy(pool_ref.at[page], dst, sem.at[slot]),)

    slot = _pipelined_gather(step, pl.num_programs(0) * steps_per_b, n_chunk, page_copy)

    @pl.when(step == 0)
    def _():
        kb[LATENT:, :] = jnp.zeros((LATX - LATENT, kb.shape[1]), BF16)

    @pl.when(c == 0)
    def _():
        m_ref[...] = jnp.full(m_ref.shape, NEG, F32)
        l_ref[...] = jnp.zeros(l_ref.shape, F32)
        acc_ref[...] = jnp.zeros(acc_ref.shape, F32)

    q1 = q1_ref[0]
    q2 = q2_ref[0]

    def update(s, values):
        m_old = m_ref[...]
        m_new = jnp.maximum(m_old, jnp.max(s, axis=-1, keepdims=True))
        alpha = jnp.exp(m_old - m_new)
        p = jnp.exp(s - m_new)
        m_ref[...] = m_new
        l_ref[...] = l_ref[...] * alpha + jnp.sum(p, axis=-1, keepdims=True)
        acc_ref[...] = acc_ref[...] * alpha + _dot_nt(p.astype(BF16), values)

    kb[0:LATENT, :] = buf[slot].astype(BF16)
    update(_dot(q1, kb[0:MLA_KVLORA, :]) + _dot(q2, kb[MLA_KVLORA:, :]), kb[0:MLA_KVLORA, :])

    @pl.when(c == steps_per_b - 1)
    def _():
        latn = latn_ref[0]
        per_b = LANE // ts
        lane_n = lax.broadcasted_iota(jnp.int32, (1, LANE), 1)
        t_row = lax.broadcasted_iota(jnp.int32, (rows, 1), 0) // MLA_HEADS
        mask = (lane_n // ts == b % per_b) & (lane_n % ts <= t_row)
        s = _dot(q1, latn[0:MLA_KVLORA]) + _dot(q2, latn[MLA_KVLORA:])
        update(jnp.where(mask, s, NEG), latn[0:MLA_KVLORA])
        o_lat = (acc_ref[...] / l_ref[...]).astype(BF16)
        full = _dot(o_lat, wuv_ref[...])
        head_row = lax.broadcasted_iota(jnp.int32, full.shape, 0) % MLA_HEADS
        head_lane = lax.broadcasted_iota(jnp.int32, full.shape, 1) // MLA_V
        full = jnp.where(head_row == head_lane, full, 0.0)
        o_ref[0] = jnp.sum(full.reshape(ts, MLA_HEADS, MLA_W), axis=1)


def _mla_sample(page_table, q1, q2, pool_t, latn, w_uv_cat, ts, n_chunk):
    bs, n_pages = page_table.shape
    rows = MLA_HEADS * ts
    per_b = LANE // ts
    grid_spec = pltpu.PrefetchScalarGridSpec(
        num_scalar_prefetch=1, grid=(bs, n_pages // n_chunk),
        in_specs=[pl.BlockSpec((1, rows, MLA_KVLORA), lambda b, c, pt: (b, 0, 0)),
                  pl.BlockSpec((1, rows, LANE), lambda b, c, pt: (b, 0, 0)),
                  pl.BlockSpec(memory_space=pl.ANY),
                  pl.BlockSpec((1, LATX, LANE), lambda b, c, pt: (0, 0, b // per_b)),
                  pl.BlockSpec(w_uv_cat.shape, lambda b, c, pt: (0, 0))],
        out_specs=pl.BlockSpec((1, ts, MLA_W), lambda b, c, pt: (b, 0, 0)),
        scratch_shapes=[pltpu.VMEM((2, LATENT, n_chunk * PAGE_SIZE), F32),
                        pltpu.VMEM((LATX, n_chunk * PAGE_SIZE), BF16),
                        pltpu.VMEM((rows, 1), F32), pltpu.VMEM((rows, 1), F32),
                        pltpu.VMEM((rows, MLA_KVLORA), F32),
                        pltpu.SemaphoreType.DMA((2,))])
    return pl.pallas_call(
        functools.partial(_mla_sample_kernel, ts=ts, n_pages=n_pages), grid_spec=grid_spec,
        out_shape=jax.ShapeDtypeStruct((bs, ts, MLA_W), F32),
        compiler_params=_cparams(("arbitrary", "arbitrary")), name="mla_sample",
    )(page_table, q1, q2, pool_t, latn, w_uv_cat)


def _prepare_weights(g_pre, w_in, nsa_w_cmp_k, nsa_w_cmp_v, mla_g_q, mla_w_uq, mla_g_kv, mla_w_uk, mla_w_uv,
                     mem_g, mem_w_kv, w_br_nsa, w_br_mla, w_br_mem, w_out, g_post):
    layers = []
    for l in range(w_in.shape[0]):
        w_row, w_t, w_uq = _front_weights(w_in[l], mla_w_uq[l])
        w_z, w_g, w_br = _back_weights(w_in[l], w_br_nsa[l], w_br_mla[l], w_br_mem[l])
        layers.append(dict(
            g_pre=g_pre[l], w_row=w_row, w_t=w_t, g_q=mla_g_q[l], w_uq=w_uq, g_kv=mla_g_kv[l], w_uk=mla_w_uk[l],
            w_bd=_compress_weights(nsa_w_cmp_k[l], nsa_w_cmp_v[l]), w_uv=mla_w_uv[l].astype(BF16),
            w_past=_past_compress_weights(nsa_w_cmp_k[l], nsa_w_cmp_v[l]),
            w_uv_cat=jnp.transpose(mla_w_uv[l], (1, 0, 2)).reshape(MLA_KVLORA, MLA_W).astype(BF16),
            mem_g=mem_g[l], mem_w_kv=mem_w_kv[l].astype(BF16), w_z=w_z, w_g=w_g, w_br=w_br,
            w_out=w_out[l].astype(BF16), g_post=g_post[l]))
    return layers


def _front_layer(x, pos, p, tm):
    names = ("q", "rc", "bg", "qmem", "qlat", "qrp", "rows_t", "win_t", "kv_t", "lat_t", "latx")
    outs = _front(x, pos, p["g_pre"], p["w_row"], p["w_t"], p["g_q"], p["w_uq"], p["g_kv"], p["w_uk"], tm)
    return dict(zip(names, outs))


def _back_layer(x, o_nsa, o_mla, o_mem, p, tm):
    flat = lambda a: a.reshape(-1, a.shape[-1])
    y = _back(flat(x), flat(o_nsa), flat(o_mla), flat(o_mem), p["g_pre"], p["w_z"], p["w_g"], p["w_br"],
              p["w_out"], p["g_post"], tm)
    return y.reshape(x.shape)


def _prompt_layer(x, mem_prompt, params, l):
    p = params[l]
    b, t, _ = x.shape
    f = _front_layer(x, jnp.arange(t), p, min(256, t))
    nb = t // NSA_BLK
    kcvc = _compress(f["rc"].reshape(b * nb, NSA_BLK * 256), p["w_bd"], NSA_BLK).reshape(b, nb, 256)
    assert nb <= LANE and t % LANE == 0
    kcvc_pad = jnp.pad(kcvc, ((0, 0), (0, LANE - nb), (0, 0)))
    o_nsa = _nsa_prompt(f["q"], f["bg"], kcvc_pad, f["kv_t"], min(256, t))
    o_mla = _mla_prompt(f["qlat"], f["qrp"], f["latx"], p["w_uv"], min(256, t))
    n_mem = mem_prompt.shape[1]
    mem_kv = _mem_kv(mem_prompt.reshape(b * n_mem, D_MODEL), p["mem_g"], p["mem_w_kv"], min(256, b * n_mem))
    mem_kv = mem_kv.reshape(b, n_mem, 2 * MEM_W)
    o_mem = _mem_attend(f["qmem"], mem_kv, min(512, t))
    y = _back_layer(x, o_nsa, o_mla, o_mem, p, min(256, b * t))
    return dict(f, kcvc=kcvc, o_nsa=o_nsa, o_mla=o_mla, mem_kv=mem_kv, o_mem=o_mem, y=y)


def _sample_layer(x, pool_t, mla_t, win_t, mem_kv, page_table, params, l):
    p = params[l]
    bs, ts, _ = x.shape
    n_tok = bs * ts
    n_pages = page_table.shape[1]
    past_len = n_pages * PAGE_SIZE
    pos = past_len + jnp.arange(n_tok) % ts
    f = _front_layer(x.reshape(1, n_tok, D_MODEL), pos, p, min(256, n_tok))
    past = _past_compress(page_table, pool_t, p["w_past"], min(64, n_pages))
    new = _compress(f["rc"].reshape(bs, ts * 256), p["w_bd"], ts)
    new = new.reshape(bs, 2, NSA_KV, 1, NSA_HD)
    nb = past.shape[3] + 1
    nbp = -(-nb // LANE) * LANE
    kcvc = jnp.concatenate([past, new, jnp.zeros((bs, 2, NSA_KV, nbp - nb, NSA_HD), F32)], axis=3)
    qg = f["q"].reshape(bs, ts, NSA_KV, NSA_REP, NSA_HD).transpose(0, 2, 3, 1, 4).reshape(bs, NSA_KV, -1, NSA_HD)
    gates = f["bg"][0, :, :NSA_HEADS * 3].reshape(bs, ts, NSA_KV, NSA_REP, 3)
    gates = gates.transpose(0, 2, 3, 1, 4).reshape(bs, NSA_KV, -1, 3)
    o_cmp, sel = _cmp_sample(qg, kcvc, ts, past_len, nb, 8)
    n_sel = min(NSA_TOPK, nb)
    sel = sel[:, :, :n_sel].reshape(bs, NSA_KV * ts * n_sel)
    o_nsa, win_new = _slc_sample(page_table, sel, qg, gates, o_cmp, pool_t, f["kv_t"], win_t, f["win_t"],
                                 ts, past_len)
    o_nsa = o_nsa.reshape(bs, NSA_KV, NSA_REP, ts, NSA_HD).transpose(0, 3, 1, 2, 4).reshape(bs, ts, NSA_W)
    q1 = f["qlat"][0].reshape(MLA_HEADS, bs, ts, MLA_KVLORA).transpose(1, 2, 0, 3).reshape(bs, -1, MLA_KVLORA)
    q2 = jnp.pad(f["qrp"].reshape(bs, ts * MLA_HEADS, MLA_ROPE), ((0, 0), (0, 0), (0, LANE - MLA_ROPE)))
    o_mla = _mla_sample(page_table, q1, q2, mla_t, f["latx"], p["w_uv_cat"], ts, min(64, n_pages))
    pad_t = 16
    qmem = jnp.pad(f["qmem"].reshape(bs, ts, MEM_W), ((0, 0), (0, pad_t - ts), (0, 0)))
    o_mem = _mem_attend_cache(qmem, mem_kv, 8)[:, :ts]
    y = _back_layer(x, o_nsa, o_mla, o_mem, p, min(256, n_tok))
    return dict(f, kcvc=kcvc, o_cmp=o_cmp, sel=sel, o_nsa=o_nsa, win_new=win_new, o_mla=o_mla, o_mem=o_mem, y=y)


def kernel(x_prompt, x_sample, cache_nsa_kv, cache_mla, cache_win_kv, cache_mem_kv, page_table, mem_prompt,
           g_pre, w_in, nsa_w_cmp_k, nsa_w_cmp_v, mla_g_q, mla_w_uq, mla_g_kv, mla_w_uk, mla_w_uv,
           mem_g, mem_w_kv, w_br_nsa, w_br_mla, w_br_mem, w_out, g_post):
    params = _prepare_weights(g_pre, w_in, nsa_w_cmp_k, nsa_w_cmp_v, mla_g_q, mla_w_uq, mla_g_kv, mla_w_uk,
                              mla_w_uv, mem_g, mem_w_kv, w_br_nsa, w_br_mla, w_br_mem, w_out, g_post)
    depth = w_in.shape[0]
    bp, tp, _ = x_prompt.shape
    bs, ts, _ = x_sample.shape
    n_pool = cache_nsa_kv.shape[1]
    wb = cache_win_kv.shape[2]
    n_mem = mem_prompt.shape[1]
    wp = min(NSA_WINDOW, tp)
    xp, xs = x_prompt, x_sample
    outs = [[] for _ in range(7)]
    for l in range(depth):
        pool_t = jnp.transpose(cache_nsa_kv[l], (0, 2, 3, 4, 1)).reshape(n_pool, 4 * NSA_KV * NSA_HD, PAGE_SIZE)
        mla_t = jnp.transpose(cache_mla[l], (0, 2, 1))
        win_t = jnp.transpose(cache_win_kv[l], (0, 2, 3, 4, 1)).reshape(bs, 2 * NSA_KV * NSA_HD, wb)
        mem_kv_s = cache_mem_kv[l].reshape(bs, n_mem * 2 * MEM_HEADS, MEM_HD)
        rp = _prompt_layer(xp, mem_prompt, params, l)
        rs = _sample_layer(xs, pool_t, mla_t, win_t, mem_kv_s, page_table, params, l)
        xp, xs = rp["y"], rs["y"]
        outs[0].append(rp["rows_t"].reshape(bp, 4, NSA_KV, NSA_HD, tp).transpose(0, 4, 1, 2, 3))
        outs[1].append(rs["rows_t"][0].T.reshape(bs, ts, 4, NSA_KV, NSA_HD))
        outs[2].append(rp["lat_t"].transpose(0, 2, 1))
        outs[3].append(rs["lat_t"][0].T.reshape(bs, ts, LATENT))
        outs[4].append(rp["win_t"][:, :, tp - wp:].reshape(bp, 2, NSA_KV, NSA_HD, wp).transpose(0, 4, 1, 2, 3))
        outs[5].append(rs["win_new"].reshape(bs, 2, NSA_KV, NSA_HD, wb).transpose(0, 4, 1, 2, 3))
        outs[6].append(rp["mem_kv"].reshape(bp, n_mem, 2, MEM_HEADS, MEM_HD))
    return (xp, xs) + tuple(jnp.stack(o, axis=0) for o in outs)
```

```python
import functools

import numpy as np
import jax
import jax.numpy as jnp
from jax import lax
from jax.experimental import pallas as pl
from jax.experimental.pallas import tpu as pltpu

F32 = jnp.float32
BF16 = jnp.bfloat16

D_MODEL = 1024
PAGE_SIZE = 128
EPS = 1e-6
ROPE_THETA = 500000.0
NSA_HEADS = 8
NSA_KV = 2
NSA_REP = NSA_HEADS // NSA_KV
NSA_HD = 64
NSA_ROT = NSA_HD // 4
NSA_BLK = 64
NSA_TOPK = 16
NSA_WINDOW = 512
NSA_SCALE = NSA_HD ** -0.5
MLA_HEADS = 8
MLA_QLORA = 384
MLA_KVLORA = 256
MLA_NOPE = 64
MLA_ROPE = 32
MLA_V = 64
MLA_SCALE = (MLA_NOPE + MLA_ROPE) ** -0.5
LATENT = MLA_KVLORA + MLA_ROPE
MEM_HEADS = 4
MEM_HD = 128
MEM_SCALE = MEM_HD ** -0.5
NSA_W = NSA_HEADS * NSA_HD
MLA_W = MLA_HEADS * MLA_V
MEM_W = MEM_HEADS * MEM_HD
N_BRANCH = 3
NSA_KV_W = 3 * 2 * NSA_KV * NSA_HD
IN_SPLITS = (NSA_W, NSA_KV_W, NSA_HEADS * 3, NSA_W,
             MLA_QLORA, MLA_KVLORA, MLA_ROPE, MLA_W,
             MEM_W, MEM_W, N_BRANCH * D_MODEL)
NEG = -1e30
TINY = 1e-30
SEL_BONUS = 1e3
LANE = 128
ROPE_REP = LANE // MLA_ROPE
LATX = MLA_KVLORA + LANE
VMEM_LIMIT = 56 * 1024 * 1024


def _cparams(sem):
    return pltpu.CompilerParams(dimension_semantics=sem, vmem_limit_bytes=VMEM_LIMIT)


def _dot(a, b):
    return jnp.dot(a, b, preferred_element_type=F32)


def _dot_nt(a, b):
    return lax.dot_general(a, b, (((1,), (1,)), ((), ())), preferred_element_type=F32)


def _rms_scale(x):
    return lax.rsqrt(jnp.mean(x * x, axis=-1, keepdims=True) + EPS)


def _softmax_rows(s, mask):
    s = jnp.where(mask, s, NEG)
    p = jnp.where(mask, jnp.exp(s - jnp.max(s, axis=-1, keepdims=True)), 0.0)
    return p / jnp.maximum(jnp.sum(p, axis=-1, keepdims=True), TINY)


def _softmax_rows2(sa, mask_a, sb, mask_b):
    sa = jnp.where(mask_a, sa, NEG)
    sb = jnp.where(mask_b, sb, NEG)
    m = jnp.maximum(jnp.max(sa, axis=-1, keepdims=True), jnp.max(sb, axis=-1, keepdims=True))
    pa = jnp.where(mask_a, jnp.exp(sa - m), 0.0)
    pb = jnp.where(mask_b, jnp.exp(sb - m), 0.0)
    den = jnp.maximum(jnp.sum(pa, axis=-1, keepdims=True) + jnp.sum(pb, axis=-1, keepdims=True), TINY)
    return pa / den, pb / den


def _rope_lanes(x, cs_ref, group, half):
    w = x.shape[1]
    reps = w // LANE
    cos = jnp.concatenate([cs_ref[0]] * reps, axis=1)
    sin = jnp.concatenate([cs_ref[1]] * reps, axis=1)
    lane = lax.broadcasted_iota(jnp.int32, x.shape, 1)
    up = pltpu.roll(x, w - half, 1)
    down = pltpu.roll(x, half, 1)
    rot = jnp.where(lane % group < half, up, down)
    return x * cos + rot * sin


def _front_kernel(x_ref, gpre_ref, wr_ref, wt_ref, gq_ref, wuq_ref, gkv_ref, wuk_ref,
                  csn_ref, csm_ref, tn_ref, tm_ref,
                  q_ref, rc_ref, bg_ref, qmem_ref, qlat_ref, qrp_ref,
                  rows_t_ref, win_t_ref, kv_t_ref, lat_t_ref, latx_ref):
    x = x_ref[0]
    h = (x * _rms_scale(x) * gpre_ref[...]).astype(BF16)
    yr = _dot(h, wr_ref[...])
    q = _rope_lanes(yr[:, 0:NSA_W], csn_ref, NSA_HD, NSA_ROT // 2)
    q_ref[0] = (q * NSA_SCALE).astype(BF16)
    kc = _rope_lanes(yr[:, 512:640], csn_ref, NSA_HD, NSA_ROT // 2)
    rc_ref[0] = jnp.concatenate([kc, yr[:, 640:768]], axis=1).astype(BF16)
    cq = yr[:, 768:1152]
    qmem_ref[0] = yr[:, 1152:1664].astype(BF16)
    bg_ref[0] = jax.nn.sigmoid(yr[:, 1664:1792])
    cqn = (cq * _rms_scale(cq) * gq_ref[...]).astype(BF16)
    qm = _dot(cqn, wuq_ref[...])
    qrp_ref[0] = (_rope_lanes(qm[:, 512:768], csm_ref, MLA_ROPE, MLA_ROPE // 2) * MLA_SCALE).astype(BF16)
    for hd in range(MLA_HEADS):
        qn = qm[:, hd * MLA_NOPE:(hd + 1) * MLA_NOPE].astype(BF16)
        qlat_ref[0, hd] = (_dot(qn, wuk_ref[hd]) * MLA_SCALE).astype(BF16)
    yt = _dot_nt(wt_ref[...], h)
    cn, sn = tn_ref[0], tn_ref[1]
    half = NSA_ROT // 2
    pieces = []
    for base in range(0, NSA_KV_W, NSA_HD):
        is_key = (base // (NSA_KV * NSA_HD)) % 2 == 0
        if is_key:
            x1 = yt[base:base + half]
            x2 = yt[base + half:base + 2 * half]
            pieces += [x1 * cn - x2 * sn, x2 * cn + x1 * sn, yt[base + 2 * half:base + NSA_HD]]
        else:
            pieces.append(yt[base:base + NSA_HD])
    kvt = jnp.concatenate(pieces, axis=0)
    rows_t_ref[0] = kvt[0:512]
    win_t_ref[0] = kvt[512:768]
    kv_t_ref[0] = kvt[256:768].astype(BF16)
    ckv = yt[768:1024]
    ckvn = ckv * lax.rsqrt(jnp.mean(ckv * ckv, axis=0, keepdims=True) + EPS) * gkv_ref[...]
    cm, sm = tm_ref[0], tm_ref[1]
    k1 = yt[1024:1040]
    k2 = yt[1040:1056]
    kr = jnp.concatenate([k1 * cm - k2 * sm, k2 * cm + k1 * sm], axis=0)
    lat_t_ref[0] = jnp.concatenate([ckvn, kr], axis=0)
    krb = kr.astype(BF16)
    latx_ref[0] = jnp.concatenate([ckvn.astype(BF16)] + [krb] * ROPE_REP, axis=0)


def _rope_tables(pos):
    pos = pos.astype(F32)

    def tables(n_rot, group):
        half = n_rot // 2
        inv = ROPE_THETA ** (-jnp.arange(half, dtype=F32) / half)
        ang = pos[:, None] * inv[None, :]
        cos, sin = jnp.cos(ang), jnp.sin(ang)
        t = pos.shape[0]
        pad1 = jnp.ones((t, group - n_rot), F32)
        pad0 = jnp.zeros((t, group - n_rot), F32)
        cos_g = jnp.concatenate([cos, cos, pad1], axis=1)
        sin_g = jnp.concatenate([-sin, sin, pad0], axis=1)
        reps = LANE // group
        row = jnp.stack([jnp.tile(cos_g, (1, reps)), jnp.tile(sin_g, (1, reps))])
        col = jnp.stack([cos.T, sin.T])
        return row, col

    csn, tn = tables(NSA_ROT, NSA_HD)
    csm, tm = tables(MLA_ROPE, MLA_ROPE)
    return csn, csm, tn, tm


def _front_weights(w_in, mla_w_uq):
    o = np.cumsum((0,) + IN_SPLITS)
    w_q, w_kv, w_bg = w_in[:, o[0]:o[1]], w_in[:, o[1]:o[2]], w_in[:, o[2]:o[3]]
    w_cq, w_ckv, w_kr = w_in[:, o[4]:o[5]], w_in[:, o[5]:o[6]], w_in[:, o[6]:o[7]]
    w_qmem = w_in[:, o[8]:o[9]]
    pad = jnp.zeros((D_MODEL, LANE - NSA_HEADS * 3), F32)
    w_row = jnp.concatenate([w_q, w_kv[:, 0:256], w_cq, w_qmem, w_bg, pad], axis=1).astype(BF16)
    w_t = jnp.concatenate([w_kv, w_ckv, w_kr], axis=1).T.astype(BF16)
    wu = mla_w_uq.reshape(MLA_QLORA, MLA_HEADS, MLA_NOPE + MLA_ROPE)
    w_uq = jnp.concatenate([wu[:, :, :MLA_NOPE].reshape(MLA_QLORA, -1),
                            wu[:, :, MLA_NOPE:].reshape(MLA_QLORA, -1)], axis=1).astype(BF16)
    return w_row, w_t, w_uq


def _front(x, pos, g_pre, w_row, w_t, g_q, w_uq, g_kv, w_uk, tm):
    b, t, _ = x.shape
    csn, csm, tn, tmm = _rope_tables(pos)
    nt = t // tm
    tok = lambda w: pl.BlockSpec((1, tm, w), lambda i, j: (i, j, 0))
    tlay = lambda r: pl.BlockSpec((1, r, tm), lambda i, j: (i, 0, j))
    full = lambda a: pl.BlockSpec(a.shape, lambda i, j: (0,) * a.ndim)
    g_pre2, g_q2 = g_pre.reshape(1, -1), g_q.reshape(1, -1)
    g_kv2 = g_kv.reshape(-1, 1)
    w_uk = w_uk.astype(BF16)
    in_specs = [tok(D_MODEL), full(g_pre2), full(w_row), full(w_t), full(g_q2), full(w_uq), full(g_kv2),
                full(w_uk),
                pl.BlockSpec((2, tm, LANE), lambda i, j: (0, j, 0)),
                pl.BlockSpec((2, tm, LANE), lambda i, j: (0, j, 0)),
                pl.BlockSpec((2, NSA_ROT // 2, tm), lambda i, j: (0, 0, j)),
                pl.BlockSpec((2, MLA_ROPE // 2, tm), lambda i, j: (0, 0, j))]
    out_shape = [
        jax.ShapeDtypeStruct((b, t, NSA_W), BF16),
        jax.ShapeDtypeStruct((b, t, 256), BF16),
        jax.ShapeDtypeStruct((b, t, LANE), F32),
        jax.ShapeDtypeStruct((b, t, MEM_W), BF16),
        jax.ShapeDtypeStruct((b, MLA_HEADS, t, MLA_KVLORA), BF16),
        jax.ShapeDtypeStruct((b, t, MLA_HEADS * MLA_ROPE), BF16),
        jax.ShapeDtypeStruct((b, 512, t), F32),
        jax.ShapeDtypeStruct((b, 256, t), F32),
        jax.ShapeDtypeStruct((b, 512, t), BF16),
        jax.ShapeDtypeStruct((b, LATENT, t), F32),
        jax.ShapeDtypeStruct((b, LATX, t), BF16),
    ]
    out_specs = [tok(NSA_W), tok(256), tok(LANE), tok(MEM_W),
                 pl.BlockSpec((1, MLA_HEADS, tm, MLA_KVLORA), lambda i, j: (i, 0, j, 0)),
                 tok(MLA_HEADS * MLA_ROPE),
                 tlay(512), tlay(256), tlay(512), tlay(LATENT), tlay(LATX)]
    return pl.pallas_call(
        _front_kernel, grid=(b, nt), in_specs=in_specs, out_specs=out_specs, out_shape=out_shape,
        compiler_params=_cparams(("parallel", "parallel")), name="front",
    )(x, g_pre2, w_row, w_t, g_q2, w_uq, g_kv2, w_uk, csn, csm, tn, tmm)


def _compress_weights(wk, wv):
    z = jnp.zeros_like(wk)
    rows = [jnp.concatenate(r, axis=2) for r in ([wk, z, z, z], [z, wk, z, z], [z, z, wv, z], [z, z, z, wv])]
    return jnp.concatenate(rows, axis=1).astype(BF16)


def _compress_kernel(x_ref, w_ref, o_ref):
    @pl.when(pl.program_id(0) == 0)
    def _():
        o_ref[...] = jnp.zeros_like(o_ref)

    o_ref[...] += _dot(x_ref[...], w_ref[0])


def _compress(x, w_bd, n_l):
    r = x.shape[0]
    return pl.pallas_call(
        _compress_kernel, grid=(n_l,),
        in_specs=[pl.BlockSpec((r, 256), lambda l: (0, l)),
                  pl.BlockSpec((1, 256, 256), lambda l: (l, 0, 0))],
        out_specs=pl.BlockSpec((r, 256), lambda l: (0, 0)),
        out_shape=jax.ShapeDtypeStruct((r, 256), F32),
        compiler_params=_cparams(("arbitrary",)), name="nsa_compress",
    )(x, w_bd)


def _select_blocks(score, nb, n_sel):
    r = score.shape[0]
    st = score.T[0:nb]
    blk = lax.broadcasted_iota(jnp.int32, (nb, 1), 0)
    rank = jnp.zeros((nb, r), F32)
    for m in range(nb):
        sm = st[m:m + 1]
        rank = rank + jnp.where((sm > st) | ((sm == st) & (m < blk)), 1.0, 0.0)
    sel_t = jnp.where(rank < n_sel, 1.0, 0.0)
    return jnp.concatenate([sel_t, jnp.zeros((LANE - nb, r), F32)], axis=0).T


ATT_KC = 512


def _lane_tiles(x):
    return [x[:, j * LANE:(j + 1) * LANE] for j in range(x.shape[1] // LANE)]


def _tile_max(x):
    return functools.reduce(jnp.maximum, _lane_tiles(x))


def _nsa_prompt_kernel(q_ref, bg_ref, kcvc_ref, kv_ref, o_ref, bias_ref, s_buf, mx_ref, l_ref, acc_ref,
                       *, tq, t_len):
    nb = t_len // NSA_BLK
    kc = min(ATT_KC, t_len)
    q0 = pl.program_id(1) * tq
    n_kc = (q0 + tq + kc - 1) // kc
    qpos = q0 + lax.broadcasted_iota(jnp.int32, (tq, 1), 0)
    blk = lax.broadcasted_iota(jnp.int32, (1, LANE), 1)
    vis = ((blk + 1) * NSA_BLK <= qpos + 1) & (blk < nb)
    cur = qpos // NSA_BLK
    valid = (blk <= cur) & (blk < nb)
    forced = (blk == 0) | (blk == cur) | (blk == cur - 1)
    key = lax.broadcasted_iota(jnp.int32, (1, t_len), 1)
    expand = (lax.broadcasted_iota(jnp.int32, (LANE, t_len), 1) // NSA_BLK
              == lax.broadcasted_iota(jnp.int32, (LANE, t_len), 0)).astype(BF16)
    wlen = min(NSA_WINDOW + tq, t_len)
    wstart = pl.multiple_of(jnp.maximum(q0 - NSA_WINDOW, 0), LANE)
    wdist = qpos - (wstart + lax.broadcasted_iota(jnp.int32, (1, wlen), 1))
    wbias = jnp.where((wdist >= 0) & (wdist < NSA_WINDOW), 0.0, NEG)
    q = q_ref[0]
    gates = bg_ref[0]
    kcvc = kcvc_ref[0].astype(BF16)
    qs = [q[:, h * NSA_HD:(h + 1) * NSA_HD] for h in range(NSA_HEADS)]
    o_cmp = []
    for g in range(NSA_KV):
        kcg = kcvc[:, g * NSA_HD:(g + 1) * NSA_HD]
        vcg = kcvc[:, 128 + g * NSA_HD:128 + (g + 1) * NSA_HD]
        psum = jnp.zeros((tq, LANE), F32)
        for r in range(NSA_REP):
            p = _softmax_rows(_dot_nt(qs[g * NSA_REP + r], kcg), vis)
            o_cmp.append(_dot(p.astype(BF16), vcg))
            psum = psum + p
        score = jnp.where(valid, psum + jnp.where(forced, SEL_BONUS, 0.0), -jnp.inf)
        sel = _select_blocks(score, nb, min(NSA_TOPK, nb))
        bias_ref[g] = jnp.where((_dot(sel.astype(BF16), expand) > 0.5) & (key <= qpos), 0.0, NEG)
    mx_ref[...] = jnp.full(mx_ref.shape, NEG, F32)
    l_ref[...] = jnp.zeros(l_ref.shape, F32)
    acc_ref[...] = jnp.zeros(acc_ref.shape, F32)

    def pass1(c, carry):
        k0 = pl.multiple_of(c * kc, kc)
        for h in range(NSA_HEADS):
            g = h // NSA_REP
            s = _dot(qs[h], kv_ref[0, g * NSA_HD:(g + 1) * NSA_HD, pl.ds(k0, kc)]) + bias_ref[g, :, pl.ds(k0, kc)]
            s_buf[h, :, pl.ds(k0, kc)] = s
            mx_ref[h] = jnp.maximum(mx_ref[h], _tile_max(s))
        return carry

    lax.fori_loop(0, n_kc, pass1, 0)
    for h in range(NSA_HEADS):
        mx_ref[h] = jnp.broadcast_to(jnp.max(mx_ref[h], axis=1, keepdims=True), (tq, LANE))

    def pass2(c, carry):
        k0 = pl.multiple_of(c * kc, kc)
        for h in range(NSA_HEADS):
            g = h // NSA_REP
            m = mx_ref[h]
            ps = [jnp.exp(t - m) for t in _lane_tiles(s_buf[h, :, pl.ds(k0, kc)])]
            l_ref[h] += functools.reduce(jnp.add, ps)
            p = jnp.concatenate(ps, axis=1).astype(BF16)
            acc_ref[h] += _dot_nt(p, kv_ref[0, 128 + g * NSA_HD:128 + (g + 1) * NSA_HD, pl.ds(k0, kc)])
        return carry

    lax.fori_loop(0, n_kc, pass2, 0)
    outs = []
    for h in range(NSA_HEADS):
        g = h // NSA_REP
        o_slc = acc_ref[h] / jnp.sum(l_ref[h], axis=1, keepdims=True)
        kw = kv_ref[0, 256 + g * NSA_HD:256 + (g + 1) * NSA_HD, pl.ds(wstart, wlen)]
        vw = kv_ref[0, 384 + g * NSA_HD:384 + (g + 1) * NSA_HD, pl.ds(wstart, wlen)]
        s = _dot(qs[h], kw) + wbias
        p = jnp.exp(s - jnp.max(s, axis=1, keepdims=True))
        o_win = _dot_nt(p.astype(BF16), vw) / jnp.sum(p, axis=1, keepdims=True)
        outs.append(gates[:, 3 * h:3 * h + 1] * o_cmp[h] + gates[:, 3 * h + 1:3 * h + 2] * o_slc
                    + gates[:, 3 * h + 2:3 * h + 3] * o_win)
    o_ref[0] = jnp.concatenate(outs, axis=1)


def _nsa_prompt(q, bg, kcvc, kv_t, tq):
    b, t, _ = q.shape
    tok = lambda w: pl.BlockSpec((1, tq, w), lambda i, j: (i, j, 0))
    return pl.pallas_call(
        functools.partial(_nsa_prompt_kernel, tq=tq, t_len=t), grid=(b, t // tq),
        in_specs=[tok(NSA_W), tok(LANE),
                  pl.BlockSpec((1, LANE, 256), lambda i, j: (i, 0, 0)),
                  pl.BlockSpec((1, 512, t), lambda i, j: (i, 0, 0))],
        out_specs=tok(NSA_W),
        out_shape=jax.ShapeDtypeStruct((b, t, NSA_W), F32),
        scratch_shapes=[pltpu.VMEM((NSA_KV, tq, t), F32), pltpu.VMEM((NSA_HEADS, tq, t), F32),
                        pltpu.VMEM((NSA_HEADS, tq, LANE), F32), pltpu.VMEM((NSA_HEADS, tq, LANE), F32),
                        pltpu.VMEM((NSA_HEADS, tq, NSA_HD), F32)],
        compiler_params=_cparams(("parallel", "parallel")), name="nsa_prompt",
    )(q, bg, kcvc, kv_t)


def _stack_rope_queries(qrp):
    r = qrp.shape[0]
    slot = lax.broadcasted_iota(jnp.int32, (r, LANE), 1) // MLA_ROPE
    parts = []
    for h in range(MLA_HEADS):
        tile = qrp[:, (h // ROPE_REP) * LANE:(h // ROPE_REP + 1) * LANE]
        parts.append(jnp.where(slot == h % ROPE_REP, tile, jnp.zeros_like(tile)))
    return jnp.concatenate(parts, axis=0)


def _mla_prompt_kernel(qlat_ref, qrp_ref, latx_ref, wuv_ref, o_ref, s_buf, mx_ref, l_ref, acc_ref, *, tq, t_len):
    kc = min(ATT_KC, t_len)
    rows = MLA_HEADS * tq
    q0 = pl.program_id(1) * tq
    n_full = q0 // kc
    q1 = qlat_ref[0].reshape(rows, MLA_KVLORA)
    q2 = _stack_rope_queries(qrp_ref[0])

    def scores(c):
        k0 = pl.multiple_of(c * kc, kc)
        return (_dot(q1, latx_ref[0, 0:MLA_KVLORA, pl.ds(k0, kc)])
                + _dot(q2, latx_ref[0, MLA_KVLORA:, pl.ds(k0, kc)]))

    mx_ref[...] = jnp.full(mx_ref.shape, NEG, F32)
    l_ref[...] = jnp.zeros(l_ref.shape, F32)
    acc_ref[...] = jnp.zeros(acc_ref.shape, F32)

    def pass1(c, carry):
        s = scores(c)
        s_buf[:, pl.ds(pl.multiple_of(c * kc, kc), kc)] = s
        mx_ref[...] = jnp.maximum(mx_ref[...], _tile_max(s))
        return carry

    lax.fori_loop(0, n_full, pass1, 0)
    qpos = q0 + lax.broadcasted_iota(jnp.int32, (rows, 1), 0) % tq
    key = n_full * kc + lax.broadcasted_iota(jnp.int32, (1, kc), 1)
    s = scores(n_full) + jnp.where(key <= qpos, 0.0, NEG)
    s_buf[:, pl.ds(pl.multiple_of(n_full * kc, kc), kc)] = s
    m = jnp.max(jnp.maximum(mx_ref[...], _tile_max(s)), axis=1, keepdims=True)
    mx_ref[...] = jnp.broadcast_to(m, (rows, LANE))

    def pass2(c, carry):
        k0 = pl.multiple_of(c * kc, kc)
        mm = mx_ref[...]
        ps = [jnp.exp(t - mm) for t in _lane_tiles(s_buf[:, pl.ds(k0, kc)])]
        l_ref[...] += functools.reduce(jnp.add, ps)
        p = jnp.concatenate(ps, axis=1).astype(BF16)
        acc_ref[...] += _dot_nt(p, latx_ref[0, 0:MLA_KVLORA, pl.ds(k0, kc)])
        return carry

    lax.fori_loop(0, n_full + 1, pass2, 0)
    o_lat = (acc_ref[...] / jnp.sum(l_ref[...], axis=1, keepdims=True)).astype(BF16)
    o_ref[0] = jnp.concatenate([_dot(o_lat[h * tq:(h + 1) * tq], wuv_ref[h]) for h in range(MLA_HEADS)], axis=1)


def _mla_prompt(qlat, qrp, latx, w_uv, tq):
    b, _, t, _ = qlat.shape
    return pl.pallas_call(
        functools.partial(_mla_prompt_kernel, tq=tq, t_len=t), grid=(b, t // tq),
        in_specs=[pl.BlockSpec((1, MLA_HEADS, tq, MLA_KVLORA), lambda i, j: (i, 0, j, 0)),
                  pl.BlockSpec((1, tq, 256), lambda i, j: (i, j, 0)),
                  pl.BlockSpec((1, LATX, t), lambda i, j: (i, 0, 0)),
                  pl.BlockSpec(w_uv.shape, lambda i, j: (0, 0, 0))],
        out_specs=pl.BlockSpec((1, tq, MLA_W), lambda i, j: (i, j, 0)),
        out_shape=jax.ShapeDtypeStruct((b, t, MLA_W), F32),
        scratch_shapes=[pltpu.VMEM((MLA_HEADS * tq, t), F32), pltpu.VMEM((MLA_HEADS * tq, LANE), F32),
                        pltpu.VMEM((MLA_HEADS * tq, LANE), F32), pltpu.VMEM((MLA_HEADS * tq, MLA_KVLORA), F32)],
        compiler_params=_cparams(("parallel", "parallel")), name="mla_prompt",
    )(qlat, qrp, latx, w_uv)


def _mem_kv_kernel(m_ref, g_ref, w_ref, o_ref):
    m = m_ref[...]
    o_ref[...] = _dot((m * _rms_scale(m) * g_ref[...]).astype(BF16), w_ref[...])


def _mem_kv(mem, g, w, tm):
    n = mem.shape[0]
    g2 = g.reshape(1, -1)
    return pl.pallas_call(
        _mem_kv_kernel, grid=(n // tm,),
        in_specs=[pl.BlockSpec((tm, D_MODEL), lambda i: (i, 0)),
                  pl.BlockSpec(g2.shape, lambda i: (0, 0)),
                  pl.BlockSpec(w.shape, lambda i: (0, 0))],
        out_specs=pl.BlockSpec((tm, 2 * MEM_W), lambda i: (i, 0)),
        out_shape=jax.ShapeDtypeStruct((n, 2 * MEM_W), F32),
        compiler_params=_cparams(("parallel",)), name="mem_kv",
    )(mem, g2, w)


def _mem_attend_kernel(q_ref, kv_ref, o_ref):
    q = q_ref[0]
    outs = []
    for h in range(MEM_HEADS):
        k = kv_ref[0, :, h * MEM_HD:(h + 1) * MEM_HD].astype(BF16)
        v = kv_ref[0, :, MEM_W + h * MEM_HD:MEM_W + (h + 1) * MEM_HD].astype(BF16)
        s = _dot_nt(q[:, h * MEM_HD:(h + 1) * MEM_HD], k) * MEM_SCALE
        e = jnp.exp(s - jnp.max(s, axis=-1, keepdims=True))
        p = e / jnp.sum(e, axis=-1, keepdims=True)
        outs.append(_dot(p.astype(BF16), v))
    o_ref[0] = jnp.concatenate(outs, axis=1)


def _mem_attend(q, kv, tq):
    b, t, _ = q.shape
    return pl.pallas_call(
        _mem_attend_kernel, grid=(b, t // tq),
        in_specs=[pl.BlockSpec((1, tq, MEM_W), lambda i, j: (i, j, 0)),
                  pl.BlockSpec((1,) + kv.shape[1:], lambda i, j: (i, 0, 0))],
        out_specs=pl.BlockSpec((1, tq, MEM_W), lambda i, j: (i, j, 0)),
        out_shape=jax.ShapeDtypeStruct((b, t, MEM_W), F32),
        compiler_params=_cparams(("parallel", "parallel")), name="mem_attend",
    )(q, kv)


def _mem_attend_cache_kernel(q_ref, kv_ref, o_ref):
    n_mem = kv_ref.shape[1] // (2 * MEM_HEADS)
    for bb in range(q_ref.shape[0]):
        q = q_ref[bb]
        outs = []
        for h in range(MEM_HEADS):
            k = kv_ref[bb, pl.ds(h, n_mem, stride=2 * MEM_HEADS), :].astype(BF16)
            v = kv_ref[bb, pl.ds(MEM_HEADS + h, n_mem, stride=2 * MEM_HEADS), :].astype(BF16)
            s = _dot_nt(q[:, h * MEM_HD:(h + 1) * MEM_HD], k) * MEM_SCALE
            e = jnp.exp(s - jnp.max(s, axis=-1, keepdims=True))
            p = e / jnp.sum(e, axis=-1, keepdims=True)
            outs.append(_dot(p.astype(BF16), v))
        o_ref[bb] = jnp.concatenate(outs, axis=1)


def _mem_attend_cache(q, kv, n_b):
    b, t, _ = q.shape
    return pl.pallas_call(
        _mem_attend_cache_kernel, grid=(b // n_b,),
        in_specs=[pl.BlockSpec((n_b, t, MEM_W), lambda i: (i, 0, 0)),
                  pl.BlockSpec((n_b,) + kv.shape[1:], lambda i: (i, 0, 0))],
        out_specs=pl.BlockSpec((n_b, t, MEM_W), lambda i: (i, 0, 0)),
        out_shape=jax.ShapeDtypeStruct((b, t, MEM_W), F32),
        compiler_params=_cparams(("parallel",)), name="mem_attend_cache",
    )(q, kv)


def _back_kernel(x_ref, on_ref, om_ref, oe_ref, gpre_ref, wz_ref, wg_ref, wbr_ref, wout_ref, gpost_ref, y_ref):
    x = x_ref[...]
    h = (x * _rms_scale(x) * gpre_ref[...]).astype(BF16)
    y = jnp.zeros(x.shape, F32)
    for j, o_ref in enumerate((on_ref, om_ref, oe_ref)):
        z = _dot(h, wz_ref[j])
        u = (o_ref[...] * (z * jax.nn.sigmoid(z))).astype(BF16)
        gate = jax.nn.sigmoid(_dot(h, wg_ref[j]))
        y = y + gate * _dot(u, wbr_ref[j])
    v = _dot(y.astype(BF16), wout_ref[...])
    y_ref[...] = x + v * _rms_scale(v) * gpost_ref[...]


def _back_weights(w_in, w_br_nsa, w_br_mla, w_br_mem):
    o = np.cumsum((0,) + IN_SPLITS)
    w_z = jnp.stack([w_in[:, o[3]:o[4]], w_in[:, o[7]:o[8]], w_in[:, o[9]:o[10]]]).astype(BF16)
    w_g = jnp.stack([w_in[:, o[10] + j * D_MODEL:o[10] + (j + 1) * D_MODEL] for j in range(N_BRANCH)]).astype(BF16)
    w_br = jnp.stack([w_br_nsa, w_br_mla, w_br_mem]).astype(BF16)
    return w_z, w_g, w_br


def _back(x, o_nsa, o_mla, o_mem, g_pre, w_z, w_g, w_br, w_out, g_post, tm):
    n = x.shape[0]
    g_pre2, g_post2 = g_pre.reshape(1, -1), g_post.reshape(1, -1)
    full = lambda a: pl.BlockSpec(a.shape, lambda i: (0,) * a.ndim)
    tok = lambda w: pl.BlockSpec((tm, w), lambda i: (i, 0))
    return pl.pallas_call(
        _back_kernel, grid=(n // tm,),
        in_specs=[tok(D_MODEL), tok(512), tok(512), tok(512), full(g_pre2), full(w_z), full(w_g), full(w_br),
                  full(w_out), full(g_post2)],
        out_specs=tok(D_MODEL),
        out_shape=jax.ShapeDtypeStruct((n, D_MODEL), F32),
        compiler_params=_cparams(("parallel",)), name="back",
    )(x, o_nsa, o_mla, o_mem, g_pre2, w_z, w_g, w_br, w_out, g_post2)


def _div(a, b):
    return lax.div(a, jnp.int32(b))


def _rem(a, b):
    return lax.rem(a, jnp.int32(b))


def _pipelined_gather(step, n_steps, n_copies, make_copies):
    slot = _rem(step, 2)

    def run(s, sl, start):
        def body(i, carry):
            for n, cp in enumerate(make_copies(s, i, sl)):
                if start:
                    cp.start(priority=n % 2)
                else:
                    cp.wait()
            return carry
        lax.fori_loop(0, n_copies, body, 0)

    @pl.when(step == 0)
    def _():
        run(step, slot, True)

    @pl.when(step + 1 < n_steps)
    def _():
        run(step + 1, 1 - slot, True)

    run(step, slot, False)
    return slot


def _past_compress_weights(wk, wv):
    def one(w):
        return jnp.einsum('jk,lde->djlke', jnp.eye(2, dtype=F32), w).reshape(NSA_HD * PAGE_SIZE, 128)
    return jnp.stack([one(wk), one(wv)]).astype(BF16)


def _past_compress_kernel(pt_ref, pool_ref, w_ref, o_ref, buf, a_buf, sem, *, n_pages):
    chunk = buf.shape[2]
    steps_per_b = n_pages // chunk
    step = pl.program_id(0) * steps_per_b + pl.program_id(1)

    def page_copy(s, i, slot):
        out = []
        for j in (2 * i, 2 * i + 1):
            page = pt_ref[_div(s, steps_per_b), _rem(s, steps_per_b) * chunk + j]
            out.append(pltpu.make_async_copy(pool_ref.at[page, pl.ds(0, 256), :], buf.at[slot, :, j, :],
                                             sem.at[slot]))
        return out

    slot = _pipelined_gather(step, pl.num_programs(0) * steps_per_b, chunk // 2, page_copy)

    for kv in range(2):
        for g in range(NSA_KV):
            for d in range(NSA_HD):
                a_buf[g * chunk:(g + 1) * chunk, d * PAGE_SIZE:(d + 1) * PAGE_SIZE] = (
                    buf[slot, kv * 128 + g * NSA_HD + d].astype(BF16))
        acc = _dot(a_buf[...], w_ref[kv])
        for g in range(NSA_KV):
            o_ref[0, kv, g] = acc[g * chunk:(g + 1) * chunk]


def _past_compress(page_table, pool_t, w2, chunk):
    bs, n_pages = page_table.shape
    grid_spec = pltpu.PrefetchScalarGridSpec(
        num_scalar_prefetch=1, grid=(bs, n_pages // chunk),
        in_specs=[pl.BlockSpec(memory_space=pl.ANY),
                  pl.BlockSpec(w2.shape, lambda b, c, pt: (0, 0, 0))],
        out_specs=pl.BlockSpec((1, 2, NSA_KV, chunk, 128), lambda b, c, pt: (b, 0, 0, c, 0)),
        scratch_shapes=[pltpu.VMEM((2, 256, chunk, PAGE_SIZE), F32),
                        pltpu.VMEM((NSA_KV * chunk, NSA_HD * PAGE_SIZE), BF16),
                        pltpu.SemaphoreType.DMA((2,))])
    out = pl.pallas_call(
        functools.partial(_past_compress_kernel, n_pages=n_pages), grid_spec=grid_spec,
        out_shape=jax.ShapeDtypeStruct((bs, 2, NSA_KV, n_pages, 128), F32),
        compiler_params=_cparams(("arbitrary", "arbitrary")), name="past_compress",
    )(page_table, pool_t, w2)
    return out.reshape(bs, 2, NSA_KV, n_pages * (PAGE_SIZE // NSA_BLK), NSA_HD)


def _cmp_sample_kernel(q_ref, kcvc_ref, oc_ref, sel_ref, sc_ref, *, ts, past_len, nb):
    nbp = kcvc_ref.shape[3]
    n_b = q_ref.shape[0]
    rows = NSA_REP * ts
    t_row = lax.broadcasted_iota(jnp.int32, (rows, 1), 0) % ts
    blk = lax.broadcasted_iota(jnp.int32, (1, nbp), 1)
    vis = ((blk + 1) * NSA_BLK <= past_len + t_row + 1) & (blk < nb)
    for bb in range(n_b):
        for g in range(NSA_KV):
            kc = kcvc_ref[bb, 0, g].astype(BF16)
            vc = kcvc_ref[bb, 1, g].astype(BF16)
            p = _softmax_rows(_dot_nt(q_ref[bb, g], kc), vis)
            oc_ref[bb, g] = _dot(p.astype(BF16), vc)
            psum = p[0:ts]
            for r in range(1, NSA_REP):
                psum = psum + p[r * ts:(r + 1) * ts]
            sc_ref[(bb * NSA_KV + g) * ts:(bb * NSA_KV + g + 1) * ts, :] = psum
    n_rows = n_b * NSA_KV * ts
    t_sel = lax.broadcasted_iota(jnp.int32, (n_rows, 1), 0) % ts
    cur = (past_len + t_sel) // NSA_BLK
    valid = (blk <= cur) & (blk < nb)
    forced = (blk == 0) | (blk == cur) | (blk == cur - 1)
    score = jnp.where(valid, sc_ref[...] + jnp.where(forced, SEL_BONUS, 0.0), -jnp.inf)
    blk_f = blk.astype(F32)
    taken = jnp.broadcast_to(blk >= nb, score.shape)
    col = lax.broadcasted_iota(jnp.int32, (n_rows, LANE), 1)
    sel = jnp.zeros((n_rows, LANE), F32)
    for k in range(min(NSA_TOPK, nb)):
        best = jnp.max(jnp.where(taken, -jnp.inf, score), axis=1, keepdims=True)
        idx = jnp.min(jnp.where((~taken) & (score == best), blk_f, float(nbp)), axis=1, keepdims=True)
        sel = jnp.where(col == k, idx, sel)
        taken = taken | (blk_f == idx)
    sel_ref[...] = sel.astype(jnp.int32).reshape(sel_ref.shape)


def _cmp_sample(qg, kcvc, ts, past_len, nb, n_b):
    bs = qg.shape[0]
    rows = NSA_REP * ts
    nbp = kcvc.shape[3]
    return pl.pallas_call(
        functools.partial(_cmp_sample_kernel, ts=ts, past_len=past_len, nb=nb), grid=(bs // n_b,),
        in_specs=[pl.BlockSpec((n_b, NSA_KV, rows, NSA_HD), lambda b: (b, 0, 0, 0)),
                  pl.BlockSpec((n_b, 2, NSA_KV, nbp, NSA_HD), lambda b: (b, 0, 0, 0, 0))],
        out_specs=[pl.BlockSpec((n_b, NSA_KV, rows, NSA_HD), lambda b: (b, 0, 0, 0)),
                   pl.BlockSpec((n_b, NSA_KV * ts, LANE), lambda b: (b, 0, 0))],
        out_shape=[jax.ShapeDtypeStruct((bs, NSA_KV, rows, NSA_HD), F32),
                   jax.ShapeDtypeStruct((bs, NSA_KV * ts, LANE), jnp.int32)],
        scratch_shapes=[pltpu.VMEM((n_b * NSA_KV * ts, nbp), F32)],
        compiler_params=_cparams(("parallel",)), name="cmp_sample",
    )(qg, kcvc)


def _slc_sample_kernel(pt_ref, sel_ref, q_ref, gate_ref, oc_ref, pool_ref, kvn_ref, win_ref, winn_ref,
                       o_ref, wout_ref, kvbuf, sem, *, ts, past_len, n_sel):
    b = pl.program_id(0)
    nbs = pl.num_programs(0)
    bpp = PAGE_SIZE // NSA_BLK
    nbp = pt_ref.shape[1] * bpp
    n_tg = NSA_KV * ts
    rows = NSA_REP * ts
    per_b = LANE // ts

    def copies(bb, gt, slot):
        krow = pl.multiple_of(256 + _div(gt, ts) * NSA_HD, NSA_HD)
        vrow = pl.multiple_of(krow + NSA_KV * NSA_HD, NSA_HD)
        out = []
        for k in range(n_sel):
            s = jnp.minimum(sel_ref[bb, gt * n_sel + k], nbp - 1)
            page = pt_ref[bb, _div(s, bpp)]
            dst = pl.ds(k * PAGE_SIZE, PAGE_SIZE)
            out.append(pltpu.make_async_copy(pool_ref.at[page, pl.ds(krow, NSA_HD), :],
                                             kvbuf.at[slot, gt, 0:NSA_HD, dst], sem.at[slot]))
            out.append(pltpu.make_async_copy(pool_ref.at[page, pl.ds(vrow, NSA_HD), :],
                                             kvbuf.at[slot, gt, NSA_HD:, dst], sem.at[slot]))
        return out

    slot = _pipelined_gather(b, nbs, n_tg, copies)

    t_row = lax.broadcasted_iota(jnp.int32, (rows, 1), 0) % ts
    lane_p = lax.broadcasted_iota(jnp.int32, (1, n_sel * PAGE_SIZE), 1)
    lane_n = lax.broadcasted_iota(jnp.int32, (1, LANE), 1)
    mine = lane_n // ts == b % per_b
    t_key = lane_n % ts
    wb = win_ref.shape[2]
    lane_w = lax.broadcasted_iota(jnp.int32, (1, wb), 1)
    wdist = t_row + wb - lane_w
    wmask = (wdist >= 0) & (wdist < NSA_WINDOW)
    wdist_n = t_row - t_key
    wmask_n = mine & (wdist_n >= 0) & (wdist_n < NSA_WINDOW)
    for g in range(NSA_KV):
        q = q_ref[0, g]
        gates = gate_ref[0, g]
        kn = kvn_ref[0, g * NSA_HD:(g + 1) * NSA_HD, :]
        vn = kvn_ref[0, 128 + g * NSA_HD:128 + (g + 1) * NSA_HD, :]
        s_new = _dot(q, kn)
        o_slc = jnp.zeros((rows, NSA_HD), F32)
        for t in range(ts):
            selv = jnp.zeros((1, n_sel * PAGE_SIZE), jnp.int32)
            chosen = jnp.zeros((1, LANE), jnp.bool_)
            for k in range(n_sel):
                s_k = sel_ref[b, (g * ts + t) * n_sel + k]
                selv = jnp.where(lane_p // PAGE_SIZE == k, s_k, selv)
                chosen = chosen | ((past_len + t_key) // NSA_BLK == s_k)
            pmask = (selv < nbp) & ((lane_p % PAGE_SIZE) // NSA_BLK == selv % bpp) & (t_row == t)
            nmask = mine & chosen & (t_key <= t_row) & (t_row == t)
            kt = kvbuf[slot, g * ts + t, 0:NSA_HD].astype(BF16)
            vt = kvbuf[slot, g * ts + t, NSA_HD:].astype(BF16)
            pa, pb = _softmax_rows2(_dot(q, kt), pmask, s_new, nmask)
            o_slc = o_slc + _dot_nt(pa.astype(BF16), vt) + _dot_nt(pb.astype(BF16), vn)
        kw = win_ref[0, g * NSA_HD:(g + 1) * NSA_HD, :].astype(BF16)
        vw = win_ref[0, 128 + g * NSA_HD:128 + (g + 1) * NSA_HD, :].astype(BF16)
        kwn = kvn_ref[0, 256 + g * NSA_HD:256 + (g + 1) * NSA_HD, :]
        vwn = kvn_ref[0, 384 + g * NSA_HD:384 + (g + 1) * NSA_HD, :]
        pa, pb = _softmax_rows2(_dot(q, kw), wmask, _dot(q, kwn), wmask_n)
        o_win = _dot_nt(pa.astype(BF16), vw) + _dot_nt(pb.astype(BF16), vwn)
        o_ref[0, g] = gates[:, 0:1] * oc_ref[0, g] + gates[:, 1:2] * o_slc + gates[:, 2:3] * o_win
    shifted = pltpu.roll(win_ref[0], wb - ts, 1)
    wout_ref[0] = shifted
    new_cols = pltpu.roll(winn_ref[0], (LANE - ts - (b % per_b) * ts) % LANE, 1)
    wout_ref[0, :, wb - LANE:] = jnp.where(lane_n >= LANE - ts, new_cols, shifted[:, wb - LANE:])


def _slc_sample(page_table, sel, qg, gates, o_cmp, pool_t, kvn_t, win_t, winn_t, ts, past_len):
    bs = qg.shape[0]
    rows = NSA_REP * ts
    n_sel = sel.shape[1] // (NSA_KV * ts)
    wb = win_t.shape[2]
    per_b = LANE // ts
    qspec = lambda w: pl.BlockSpec((1, NSA_KV, rows, w), lambda b, pt, sl: (b, 0, 0, 0))
    grid_spec = pltpu.PrefetchScalarGridSpec(
        num_scalar_prefetch=2, grid=(bs,),
        in_specs=[qspec(NSA_HD), qspec(3), qspec(NSA_HD),
                  pl.BlockSpec(memory_space=pl.ANY),
                  pl.BlockSpec((1, 512, LANE), lambda b, pt, sl: (0, 0, b // per_b)),
                  pl.BlockSpec((1, 256, wb), lambda b, pt, sl: (b, 0, 0)),
                  pl.BlockSpec((1, 256, LANE), lambda b, pt, sl: (0, 0, b // per_b))],
        out_specs=[qspec(NSA_HD), pl.BlockSpec((1, 256, wb), lambda b, pt, sl: (b, 0, 0))],
        scratch_shapes=[pltpu.VMEM((2, NSA_KV * ts, 2 * NSA_HD, n_sel * PAGE_SIZE), F32),
                        pltpu.SemaphoreType.DMA((2,))])
    return pl.pallas_call(
        functools.partial(_slc_sample_kernel, ts=ts, past_len=past_len, n_sel=n_sel), grid_spec=grid_spec,
        out_shape=[jax.ShapeDtypeStruct((bs, NSA_KV, rows, NSA_HD), F32),
                   jax.ShapeDtypeStruct((bs, 256, wb), F32)],
        compiler_params=_cparams(("arbitrary",)), name="slc_sample",
    )(page_table, sel, qg, gates, o_cmp, pool_t, kvn_t, win_t, winn_t)


def _mla_sample_kernel(pt_ref, q1_ref, q2_ref, pool_ref, latn_ref, wuv_ref, o_ref,
                       buf, kb, m_ref, l_ref, acc_ref, sem, *, ts, n_pages):
    n_chunk = buf.shape[2] // PAGE_SIZE
    steps_per_b = n_pages // n_chunk
    b, c = pl.program_id(0), pl.program_id(1)
    step = b * steps_per_b + c
    rows = MLA_HEADS * ts

    def page_copy(s, i, slot):
        out = []
        for j in (2 * i, 2 * i + 1):
            page = pt_ref[_div(s, steps_per_b), _rem(s, steps_per_b) * n_chunk + j]
            dst = buf.at[slot, :, pl.ds(pl.multiple_of(j * PAGE_SIZE, PAGE_SIZE), PAGE_SIZE)]
            out.append(pltpu.make_async_copy(pool_ref.at[page], dst, sem.at[slot]))
        return out

    slot = _pipelined_gather(step, pl.num_programs(0) * steps_per_b, n_chunk // 2, page_copy)

    @pl.when(step == 0)
    def _():
        kb[LATENT:, :] = jnp.zeros((LATX - LATENT, kb.shape[1]), BF16)

    @pl.when(c == 0)
    def _():
        m_ref[...] = jnp.full(m_ref.shape, NEG, F32)
        l_ref[...] = jnp.zeros(l_ref.shape, F32)
        acc_ref[...] = jnp.zeros(acc_ref.shape, F32)

    q1 = q1_ref[0]
    q2 = q2_ref[0]

    def update(s, values):
        m_old = m_ref[...]
        m_new = jnp.maximum(m_old, jnp.max(s, axis=-1, keepdims=True))
        alpha = jnp.exp(m_old - m_new)
        p = jnp.exp(s - m_new)
        m_ref[...] = m_new
        l_ref[...] = l_ref[...] * alpha + jnp.sum(p, axis=-1, keepdims=True)
        acc_ref[...] = acc_ref[...] * alpha + _dot_nt(p.astype(BF16), values)

    kb[0:LATENT, :] = buf[slot].astype(BF16)
    update(_dot(q1, kb[0:MLA_KVLORA, :]) + _dot(q2, kb[MLA_KVLORA:, :]), kb[0:MLA_KVLORA, :])

    @pl.when(c == steps_per_b - 1)
    def _():
        latn = latn_ref[0]
        per_b = LANE // ts
        lane_n = lax.broadcasted_iota(jnp.int32, (1, LANE), 1)
        t_row = lax.broadcasted_iota(jnp.int32, (rows, 1), 0) // MLA_HEADS
        mask = (lane_n // ts == b % per_b) & (lane_n % ts <= t_row)
        s = _dot(q1, latn[0:MLA_KVLORA]) + _dot(q2, latn[MLA_KVLORA:])
        update(jnp.where(mask, s, NEG), latn[0:MLA_KVLORA])
        o_lat = (acc_ref[...] / l_ref[...]).astype(BF16)
        full = _dot(o_lat, wuv_ref[...])
        head_row = lax.broadcasted_iota(jnp.int32, full.shape, 0) % MLA_HEADS
        head_lane = lax.broadcasted_iota(jnp.int32, full.shape, 1) // MLA_V
        full = jnp.where(head_row == head_lane, full, 0.0)
        o_ref[0] = jnp.sum(full.reshape(ts, MLA_HEADS, MLA_W), axis=1)


def _mla_sample(page_table, q1, q2, pool_t, latn, w_uv_cat, ts, n_chunk):
    bs, n_pages = page_table.shape
    rows = MLA_HEADS * ts
    per_b = LANE // ts
    grid_spec = pltpu.PrefetchScalarGridSpec(
        num_scalar_prefetch=1, grid=(bs, n_pages // n_chunk),
        in_specs=[pl.BlockSpec((1, rows, MLA_KVLORA), lambda b, c, pt: (b, 0, 0)),
                  pl.BlockSpec((1, rows, LANE), lambda b, c, pt: (b, 0, 0)),
                  pl.BlockSpec(memory_space=pl.ANY),
                  pl.BlockSpec((1, LATX, LANE), lambda b, c, pt: (0, 0, b // per_b)),
                  pl.BlockSpec(w_uv_cat.shape, lambda b, c, pt: (0, 0))],
        out_specs=pl.BlockSpec((1, ts, MLA_W), lambda b, c, pt: (b, 0, 0)),
        scratch_shapes=[pltpu.VMEM((2, LATENT, n_chunk * PAGE_SIZE), F32),
                        pltpu.VMEM((LATX, n_chunk * PAGE_SIZE), BF16),
                        pltpu.VMEM((rows, 1), F32), pltpu.VMEM((rows, 1), F32),
                        pltpu.VMEM((rows, MLA_KVLORA), F32),
                        pltpu.SemaphoreType.DMA((2,))])
    return pl.pallas_call(
        functools.partial(_mla_sample_kernel, ts=ts, n_pages=n_pages), grid_spec=grid_spec,
        out_shape=jax.ShapeDtypeStruct((bs, ts, MLA_W), F32),
        compiler_params=_cparams(("arbitrary", "arbitrary")), name="mla_sample",
    )(page_table, q1, q2, pool_t, latn, w_uv_cat)


def _prepare_weights(g_pre, w_in, nsa_w_cmp_k, nsa_w_cmp_v, mla_g_q, mla_w_uq, mla_g_kv, mla_w_uk, mla_w_uv,
                     mem_g, mem_w_kv, w_br_nsa, w_br_mla, w_br_mem, w_out, g_post):
    layers = []
    for l in range(w_in.shape[0]):
        w_row, w_t, w_uq = _front_weights(w_in[l], mla_w_uq[l])
        w_z, w_g, w_br = _back_weights(w_in[l], w_br_nsa[l], w_br_mla[l], w_br_mem[l])
        layers.append(dict(
            g_pre=g_pre[l], w_row=w_row, w_t=w_t, g_q=mla_g_q[l], w_uq=w_uq, g_kv=mla_g_kv[l], w_uk=mla_w_uk[l],
            w_bd=_compress_weights(nsa_w_cmp_k[l], nsa_w_cmp_v[l]), w_uv=mla_w_uv[l].astype(BF16),
            w_past=_past_compress_weights(nsa_w_cmp_k[l], nsa_w_cmp_v[l]),
            w_uv_cat=jnp.transpose(mla_w_uv[l], (1, 0, 2)).reshape(MLA_KVLORA, MLA_W).astype(BF16),
            mem_g=mem_g[l], mem_w_kv=mem_w_kv[l].astype(BF16), w_z=w_z, w_g=w_g, w_br=w_br,
            w_out=w_out[l].astype(BF16), g_post=g_post[l]))
    return layers


def _front_layer(x, pos, p, tm):
    names = ("q", "rc", "bg", "qmem", "qlat", "qrp", "rows_t", "win_t", "kv_t", "lat_t", "latx")
    outs = _front(x, pos, p["g_pre"], p["w_row"], p["w_t"], p["g_q"], p["w_uq"], p["g_kv"], p["w_uk"], tm)
    return dict(zip(names, outs))


def _back_layer(x, o_nsa, o_mla, o_mem, p, tm):
    flat = lambda a: a.reshape(-1, a.shape[-1])
    y = _back(flat(x), flat(o_nsa), flat(o_mla), flat(o_mem), p["g_pre"], p["w_z"], p["w_g"], p["w_br"],
              p["w_out"], p["g_post"], tm)
    return y.reshape(x.shape)


def _prompt_layer(x, mem_prompt, params, l):
    p = params[l]
    b, t, _ = x.shape
    f = _front_layer(x, jnp.arange(t), p, min(256, t))
    nb = t // NSA_BLK
    kcvc = _compress(f["rc"].reshape(b * nb, NSA_BLK * 256), p["w_bd"], NSA_BLK).reshape(b, nb, 256)
    assert nb <= LANE and t % LANE == 0
    kcvc_pad = jnp.pad(kcvc, ((0, 0), (0, LANE - nb), (0, 0)))
    o_nsa = _nsa_prompt(f["q"], f["bg"], kcvc_pad, f["kv_t"], min(256, t))
    o_mla = _mla_prompt(f["qlat"], f["qrp"], f["latx"], p["w_uv"], min(256, t))
    n_mem = mem_prompt.shape[1]
    mem_kv = _mem_kv(mem_prompt.reshape(b * n_mem, D_MODEL), p["mem_g"], p["mem_w_kv"], min(256, b * n_mem))
    mem_kv = mem_kv.reshape(b, n_mem, 2 * MEM_W)
    o_mem = _mem_attend(f["qmem"], mem_kv, min(512, t))
    y = _back_layer(x, o_nsa, o_mla, o_mem, p, min(256, b * t))
    return dict(f, kcvc=kcvc, o_nsa=o_nsa, o_mla=o_mla, mem_kv=mem_kv, o_mem=o_mem, y=y)


def _sample_layer(x, pool_t, mla_t, win_t, mem_kv, page_table, params, l):
    p = params[l]
    bs, ts, _ = x.shape
    n_tok = bs * ts
    n_pages = page_table.shape[1]
    past_len = n_pages * PAGE_SIZE
    pos = past_len + jnp.arange(n_tok) % ts
    f = _front_layer(x.reshape(1, n_tok, D_MODEL), pos, p, min(256, n_tok))
    past = _past_compress(page_table, pool_t, p["w_past"], min(64, n_pages))
    new = _compress(f["rc"].reshape(bs, ts * 256), p["w_bd"], ts)
    new = new.reshape(bs, 2, NSA_KV, 1, NSA_HD)
    nb = past.shape[3] + 1
    nbp = -(-nb // LANE) * LANE
    kcvc = jnp.concatenate([past, new, jnp.zeros((bs, 2, NSA_KV, nbp - nb, NSA_HD), F32)], axis=3)
    qg = f["q"].reshape(bs, ts, NSA_KV, NSA_REP, NSA_HD).transpose(0, 2, 3, 1, 4).reshape(bs, NSA_KV, -1, NSA_HD)
    gates = f["bg"][0, :, :NSA_HEADS * 3].reshape(bs, ts, NSA_KV, NSA_REP, 3)
    gates = gates.transpose(0, 2, 3, 1, 4).reshape(bs, NSA_KV, -1, 3)
    o_cmp, sel = _cmp_sample(qg, kcvc, ts, past_len, nb, 8)
    n_sel = min(NSA_TOPK, nb)
    sel = sel[:, :, :n_sel].reshape(bs, NSA_KV * ts * n_sel)
    o_nsa, win_new = _slc_sample(page_table, sel, qg, gates, o_cmp, pool_t, f["kv_t"], win_t, f["win_t"],
                                 ts, past_len)
    o_nsa = o_nsa.reshape(bs, NSA_KV, NSA_REP, ts, NSA_HD).transpose(0, 3, 1, 2, 4).reshape(bs, ts, NSA_W)
    q1 = f["qlat"][0].reshape(MLA_HEADS, bs, ts, MLA_KVLORA).transpose(1, 2, 0, 3).reshape(bs, -1, MLA_KVLORA)
    q2 = jnp.pad(f["qrp"].reshape(bs, ts * MLA_HEADS, MLA_ROPE), ((0, 0), (0, 0), (0, LANE - MLA_ROPE)))
    o_mla = _mla_sample(page_table, q1, q2, mla_t, f["latx"], p["w_uv_cat"], ts, min(64, n_pages))
    pad_t = 16
    qmem = jnp.pad(f["qmem"].reshape(bs, ts, MEM_W), ((0, 0), (0, pad_t - ts), (0, 0)))
    o_mem = _mem_attend_cache(qmem, mem_kv, 8)[:, :ts]
    y = _back_layer(x, o_nsa, o_mla, o_mem, p, min(256, n_tok))
    return dict(f, kcvc=kcvc, o_cmp=o_cmp, sel=sel, o_nsa=o_nsa, win_new=win_new, o_mla=o_mla, o_mem=o_mem, y=y)


def kernel(x_prompt, x_sample, cache_nsa_kv, cache_mla, cache_win_kv, cache_mem_kv, page_table, mem_prompt,
           g_pre, w_in, nsa_w_cmp_k, nsa_w_cmp_v, mla_g_q, mla_w_uq, mla_g_kv, mla_w_uk, mla_w_uv,
           mem_g, mem_w_kv, w_br_nsa, w_br_mla, w_br_mem, w_out, g_post):
    params = _prepare_weights(g_pre, w_in, nsa_w_cmp_k, nsa_w_cmp_v, mla_g_q, mla_w_uq, mla_g_kv, mla_w_uk,
                              mla_w_uv, mem_g, mem_w_kv, w_br_nsa, w_br_mla, w_br_mem, w_out, g_post)
    depth = w_in.shape[0]
    bp, tp, _ = x_prompt.shape
    bs, ts, _ = x_sample.shape
    n_pool = cache_nsa_kv.shape[1]
    wb = cache_win_kv.shape[2]
    n_mem = mem_prompt.shape[1]
    wp = min(NSA_WINDOW, tp)
    xp, xs = x_prompt, x_sample
    outs = [[] for _ in range(7)]
    for l in range(depth):
        pool_t = jnp.transpose(cache_nsa_kv[l], (0, 2, 3, 4, 1)).reshape(n_pool, 4 * NSA_KV * NSA_HD, PAGE_SIZE)
        mla_t = jnp.transpose(cache_mla[l], (0, 2, 1))
        win_t = jnp.transpose(cache_win_kv[l], (0, 2, 3, 4, 1)).reshape(bs, 2 * NSA_KV * NSA_HD, wb)
        mem_kv_s = cache_mem_kv[l].reshape(bs, n_mem * 2 * MEM_HEADS, MEM_HD)
        rp = _prompt_layer(xp, mem_prompt, params, l)
        rs = _sample_layer(xs, pool_t, mla_t, win_t, mem_kv_s, page_table, params, l)
        xp, xs = rp["y"], rs["y"]
        outs[0].append(rp["rows_t"].reshape(bp, 4, NSA_KV, NSA_HD, tp).transpose(0, 4, 1, 2, 3))
        outs[1].append(rs["rows_t"][0].T.reshape(bs, ts, 4, NSA_KV, NSA_HD))
        outs[2].append(rp["lat_t"].transpose(0, 2, 1))
        outs[3].append(rs["lat_t"][0].T.reshape(bs, ts, LATENT))
        outs[4].append(rp["win_t"][:, :, tp - wp:].reshape(bp, 2, NSA_KV, NSA_HD, wp).transpose(0, 4, 1, 2, 3))
        outs[5].append(rs["win_new"].reshape(bs, 2, NSA_KV, NSA_HD, wb).transpose(0, 4, 1, 2, 3))
        outs[6].append(rp["mem_kv"].reshape(bp, n_mem, 2, MEM_HEADS, MEM_HD))
    return (xp, xs) + tuple(jnp.stack(o, axis=0) for o in outs)
```
